```python
import math
import jax, jax.numpy as jnp
from jax import lax
import numpy as np

D_MODEL = 2048
BATCH = 2
SEQ = 4096
DEPTH = 4
DEC_BATCH = 8
DEC_SEQ = 4
PAST_LEN = 16384
PAGE_SIZE = 128

N_MIXERS = 3
N_HEADS = 16
HEAD_DIM = D_MODEL // N_HEADS
ROT_DIM = HEAD_DIM // 4
ROPE_THETA = 500000.0
N_POOL_GROUPS = 4
POOL_WINDOWS = (2, 4, 8, 16)
POOL_GROUP_DIM = D_MODEL // N_POOL_GROUPS
POOL_STATE = max(POOL_WINDOWS) - 1
N_IDX_HEADS = 16
IDX_DIM = 64
IDX_ROT_DIM = IDX_DIM // 4
TOPK_MAX = 256
IDX_SPLITS = (3 * D_MODEL, 3 * D_MODEL + N_IDX_HEADS * IDX_DIM, 3 * D_MODEL + N_IDX_HEADS * IDX_DIM + IDX_DIM)
D_IN_DSA = IDX_SPLITS[2] + N_IDX_HEADS
D_FF = -(-8 * D_MODEL // (3 * 256)) * 256
Q_BLOCK = 128
RMS_EPS = 1e-6

kernel_name = 'sb_pool_dsa_hybrid_step'


def rmsnorm(x, g):
    xf = x.astype(jnp.float32)
    y = xf * lax.rsqrt(jnp.mean(xf * xf, axis=-1, keepdims=True) + RMS_EPS)
    return (y * g.astype(jnp.float32)).astype(x.dtype)


def rope_partial(x, pos, rot_dim):
    half = rot_dim // 2
    inv_freq = ROPE_THETA ** (-jnp.arange(half, dtype=jnp.float32) / half)
    ang = pos.astype(jnp.float32)[:, None] * inv_freq[None, :]
    cos = jnp.cos(ang)[None, :, None, :]
    sin = jnp.sin(ang)[None, :, None, :]
    xr = x[..., :rot_dim].astype(jnp.float32)
    x1, x2 = xr[..., :half], xr[..., half:]
    rot = jnp.concatenate([x1 * cos - x2 * sin, x2 * cos + x1 * sin], axis=-1)
    return jnp.concatenate([rot.astype(x.dtype), x[..., rot_dim:]], axis=-1)


def gather_pages(pool, page_table):
    rows = pool[page_table]
    return rows.reshape((page_table.shape[0], page_table.shape[1] * PAGE_SIZE) + pool.shape[2:])


def gather_rows(a, idx):
    return jax.vmap(lambda a_b, i_b: a_b[i_b])(a, idx)


def swiglu(h, w_gate, w_up, w_down):
    return (jax.nn.silu(h @ w_gate) * (h @ w_up)) @ w_down


def sb_project(h, w_qkv, q_norm, k_norm):
    b, t, _ = h.shape
    qkv = (h @ w_qkv).reshape(b, t, 3, N_HEADS, HEAD_DIM)
    return rmsnorm(qkv[:, :, 0], q_norm), rmsnorm(qkv[:, :, 1], k_norm), qkv[:, :, 2]


def stick_breaking(q, k, v, q_pos, k_pos):
    z = jnp.einsum('bqhd,bkhd->bhqk', q, k).astype(jnp.float32) * HEAD_DIM ** -0.5
    vis = k_pos[None, :] < q_pos[:, None]
    log_beta = jax.nn.log_sigmoid(z)
    log_keep = jnp.where(vis, jax.nn.log_sigmoid(-z), 0.0)
    tail = lax.cumsum(log_keep, axis=3, reverse=True) - log_keep
    weights = jnp.where(vis, jnp.exp(log_beta + tail), 0.0)
    return jnp.einsum('bhqk,bkhd->bqhd', weights.astype(v.dtype), v)


def sb_prompt(h, w_qkv, q_norm, k_norm, w_o):
    b, t, _ = h.shape
    q, k, v = sb_project(h, w_qkv, q_norm, k_norm)
    pos = jnp.arange(t, dtype=jnp.int32)
    blocks = []
    for start in range(0, t, Q_BLOCK):
        end = start + Q_BLOCK
        blocks.append(stick_breaking(q[:, start:end], k[:, :end], v[:, :end], pos[start:end], pos[:end]))
    o = jnp.concatenate(blocks, axis=1).reshape(b, t, D_MODEL)
    return o @ w_o, (k, v)


def sb_sample(h, cache_k, cache_v, page_table, w_qkv, q_norm, k_norm, w_o):
    b, t, _ = h.shape
    past = page_table.shape[1] * PAGE_SIZE
    q, k, v = sb_project(h, w_qkv, q_norm, k_norm)
    k_all = jnp.concatenate([gather_pages(cache_k, page_table), k], axis=1)
    v_all = jnp.concatenate([gather_pages(cache_v, page_table), v], axis=1)
    q_pos = past + jnp.arange(t, dtype=jnp.int32)
    k_pos = jnp.arange(past + t, dtype=jnp.int32)
    o = stick_breaking(q, k_all, v_all, q_pos, k_pos).reshape(b, t, D_MODEL)
    return o @ w_o, (k, v)


def pool_mix(h, prefix, pos, w_pool, pool_scale):
    b, t, d = h.shape
    xs = jnp.concatenate([prefix, h], axis=1).astype(jnp.float32)
    csum = jnp.concatenate([jnp.zeros((b, 1, d), jnp.float32), jnp.cumsum(xs, axis=1)], axis=1)
    upto = csum[:, POOL_STATE + 1:POOL_STATE + 1 + t]
    hf = h.astype(jnp.float32)
    outs = []
    for g, win in enumerate(POOL_WINDOWS):
        c = slice(g * POOL_GROUP_DIM, (g + 1) * POOL_GROUP_DIM)
        lo = POOL_STATE + 1 - win
        wsum = upto[..., c] - csum[:, lo:lo + t, c]
        count = jnp.minimum(pos + 1, win).astype(jnp.float32)[None, :, None]
        pooled = wsum / count - hf[..., c]
        outs.append(jnp.einsum('btc,ce->bte', pooled.astype(h.dtype), w_pool[g]))
    return jnp.concatenate(outs, axis=-1) * pool_scale


def pool_prompt(h, w_pool, pool_scale):
    b, t, d = h.shape
    prefix = jnp.zeros((b, POOL_STATE, d), h.dtype)
    y = pool_mix(h, prefix, jnp.arange(t, dtype=jnp.int32), w_pool, pool_scale)
    return y, (jnp.concatenate([prefix, h], axis=1)[:, -POOL_STATE:],)


def pool_sample(h, state_pool, w_pool, pool_scale):
    t = h.shape[1]
    pos = PAST_LEN + jnp.arange(t, dtype=jnp.int32)
    y = pool_mix(h, state_pool, pos, w_pool, pool_scale)
    return y, (jnp.concatenate([state_pool, h], axis=1)[:, -POOL_STATE:],)


def dsa_project(h, w_in, q_norm, k_norm, pos):
    b, t, _ = h.shape
    qkv, iq, ik, iw = jnp.split(h @ w_in, IDX_SPLITS, axis=-1)
    qkv = qkv.reshape(b, t, 3, N_HEADS, HEAD_DIM)
    q = rope_partial(rmsnorm(qkv[:, :, 0], q_norm), pos, ROT_DIM)
    k = rope_partial(rmsnorm(qkv[:, :, 1], k_norm), pos, ROT_DIM)
    iq = rope_partial(iq.reshape(b, t, N_IDX_HEADS, IDX_DIM), pos, IDX_ROT_DIM)
    ik = rope_partial(ik.reshape(b, t, 1, IDX_DIM), pos, IDX_ROT_DIM)[:, :, 0]
    return q, k, qkv[:, :, 2], iq, ik, iw * N_IDX_HEADS ** -0.5


def indexer_scores(iq, iw, ik, q_pos, k_pos):
    dots = jnp.einsum('bqhd,bkd->bqhk', iq, ik).astype(jnp.float32) * IDX_DIM ** -0.5
    score = jnp.einsum('bqh,bqhk->bqk', iw.astype(jnp.float32), jax.nn.relu(dots))
    return jnp.where(k_pos[None, None, :] <= q_pos[None, :, None], score, -jnp.inf)


def sparse_attend(q, k_sel, v_sel, sel_pos, q_pos):
    s = jnp.einsum('bqhd,bqkhd->bhqk', q, k_sel).astype(jnp.float32) * HEAD_DIM ** -0.5
    valid = (sel_pos <= q_pos[None, :, None])[:, None]
    p = jax.nn.softmax(jnp.where(valid, s, -jnp.inf), axis=-1)
    return jnp.einsum('bhqk,bqkhd->bqhd', p.astype(v_sel.dtype), v_sel)


def dsa_prompt(h, w_in, q_norm, k_norm, w_o):
    b, t, _ = h.shape
    pos = jnp.arange(t, dtype=jnp.int32)
    q, k, v, iq, ik, iw = dsa_project(h, w_in, q_norm, k_norm, pos)
    topk = min(TOPK_MAX, t // 4)
    blocks = []
    for start in range(0, t, Q_BLOCK):
        end = start + Q_BLOCK
        scores = indexer_scores(iq[:, start:end], iw[:, start:end], ik[:, :end], pos[start:end], pos[:end])
        _, idx = lax.top_k(scores, min(topk, end))
        blocks.append(sparse_attend(q[:, start:end], gather_rows(k, idx), gather_rows(v, idx), idx, pos[start:end]))
    o = jnp.concatenate(blocks, axis=1).reshape(b, t, D_MODEL)
    return o @ w_o, (k, v, ik)


def dsa_sample(h, cache_k, cache_v, cache_kidx, page_table, w_in, q_norm, k_norm, w_o):
    b, t, _ = h.shape
    past = page_table.shape[1] * PAGE_SIZE
    q_pos = past + jnp.arange(t, dtype=jnp.int32)
    k_pos = jnp.arange(past + t, dtype=jnp.int32)
    q, k, v, iq, ik, iw = dsa_project(h, w_in, q_norm, k_norm, q_pos)
    ik_all = jnp.concatenate([gather_pages(cache_kidx, page_table), ik], axis=1)
    scores = indexer_scores(iq, iw, ik_all, q_pos, k_pos)
    _, idx = lax.top_k(scores, min(TOPK_MAX, (past + t) // 4))
    in_past = (idx < past)[..., None, None]
    pidx = jnp.minimum(idx, past - 1)
    phys = gather_rows(page_table, pidx // PAGE_SIZE)
    off = pidx % PAGE_SIZE
    nidx = jnp.clip(idx - past, 0, t - 1)
    k_sel = jnp.where(in_past, cache_k[phys, off], gather_rows(k, nidx))
    v_sel = jnp.where(in_past, cache_v[phys, off], gather_rows(v, nidx))
    o = sparse_attend(q, k_sel, v_sel, idx, q_pos).reshape(b, t, D_MODEL)
    return o @ w_o, (k, v, ik)


def setup_inputs(seed: int = 0) -> dict:
    key = jax.random.key(seed)
    keys = iter(jax.random.split(key, 128))

    def normal(shape, scale):
        return jax.random.normal(next(keys), shape, jnp.float32) * scale

    def gain(n):
        return 1.0 + normal((n,), 0.02)

    n_pages = PAST_LEN // PAGE_SIZE
    n_phys = (DEC_BATCH * n_pages * 5) // 4
    kv_shape = (n_phys, PAGE_SIZE, N_HEADS, HEAD_DIM)
    inputs = {}
    inputs['x_prompt'] = normal((BATCH, SEQ, D_MODEL), 1.0)
    inputs['x_sample'] = normal((DEC_BATCH, DEC_SEQ, D_MODEL), 1.0)
    inputs['cache_k_0'] = normal(kv_shape, 1.0)
    inputs['cache_v_0'] = normal(kv_shape, 1.0)
    inputs['state_pool_1'] = normal((DEC_BATCH, POOL_STATE, D_MODEL), 1.0)
    inputs['cache_k_2'] = normal(kv_shape, 1.0)
    inputs['cache_v_2'] = normal(kv_shape, 1.0)
    inputs['cache_kidx_2'] = normal((n_phys, PAGE_SIZE, IDX_DIM), 1.0)
    inputs['cache_k_3'] = normal(kv_shape, 1.0)
    inputs['cache_v_3'] = normal(kv_shape, 1.0)
    perm = jax.random.permutation(next(keys), n_phys)
    inputs['page_table'] = perm[:DEC_BATCH * n_pages].reshape(DEC_BATCH, n_pages).astype(jnp.int32)
    for i in range(DEPTH):
        kind = i % N_MIXERS
        inputs[f'attn_norm_{i}'] = gain(D_MODEL)
        if kind == 0:
            inputs[f'w_qkv_{i}'] = normal((D_MODEL, 3 * D_MODEL), D_MODEL ** -0.5)
            inputs[f'q_norm_{i}'] = gain(HEAD_DIM)
            inputs[f'k_norm_{i}'] = gain(HEAD_DIM)
            inputs[f'w_o_{i}'] = normal((D_MODEL, D_MODEL), D_MODEL ** -0.5)
        elif kind == 1:
            inputs[f'w_pool_{i}'] = normal((N_POOL_GROUPS, POOL_GROUP_DIM, POOL_GROUP_DIM), POOL_GROUP_DIM ** -0.5)
            inputs[f'pool_scale_{i}'] = gain(D_MODEL)
        else:
            inputs[f'w_in_{i}'] = normal((D_MODEL, D_IN_DSA), D_MODEL ** -0.5)
            inputs[f'q_norm_{i}'] = gain(HEAD_DIM)
            inputs[f'k_norm_{i}'] = gain(HEAD_DIM)
            inputs[f'w_o_{i}'] = normal((D_MODEL, D_MODEL), D_MODEL ** -0.5)
        inputs[f'ffn_norm_{i}'] = gain(D_MODEL)
        inputs[f'w_gate_{i}'] = normal((D_MODEL, D_FF), D_MODEL ** -0.5)
        inputs[f'w_up_{i}'] = normal((D_MODEL, D_FF), D_MODEL ** -0.5)
        inputs[f'w_down_{i}'] = normal((D_FF, D_MODEL), D_FF ** -0.5)
    return inputs


def reference(x_prompt, x_sample, cache_k_0, cache_v_0, state_pool_1, cache_k_2, cache_v_2, cache_kidx_2,
              cache_k_3, cache_v_3, page_table,
              attn_norm_0, w_qkv_0, q_norm_0, k_norm_0, w_o_0, ffn_norm_0, w_gate_0, w_up_0, w_down_0,
              attn_norm_1, w_pool_1, pool_scale_1, ffn_norm_1, w_gate_1, w_up_1, w_down_1,
              attn_norm_2, w_in_2, q_norm_2, k_norm_2, w_o_2, ffn_norm_2, w_gate_2, w_up_2, w_down_2,
              attn_norm_3, w_qkv_3, q_norm_3, k_norm_3, w_o_3, ffn_norm_3, w_gate_3, w_up_3, w_down_3):
    attn_norms = (attn_norm_0, attn_norm_1, attn_norm_2, attn_norm_3)
    mixer_params = ((w_qkv_0, q_norm_0, k_norm_0, w_o_0),
                    (w_pool_1, pool_scale_1),
                    (w_in_2, q_norm_2, k_norm_2, w_o_2),
                    (w_qkv_3, q_norm_3, k_norm_3, w_o_3))
    layer_caches = ((cache_k_0, cache_v_0), (state_pool_1,), (cache_k_2, cache_v_2, cache_kidx_2), (cache_k_3, cache_v_3))
    ffn_params = ((ffn_norm_0, w_gate_0, w_up_0, w_down_0), (ffn_norm_1, w_gate_1, w_up_1, w_down_1),
                  (ffn_norm_2, w_gate_2, w_up_2, w_down_2), (ffn_norm_3, w_gate_3, w_up_3, w_down_3))
    xp, xs = x_prompt, x_sample
    new_prompt, new_sample = [], []
    for i in range(DEPTH):
        kind = i % N_MIXERS
        hp = rmsnorm(xp, attn_norms[i])
        hs = rmsnorm(xs, attn_norms[i])
        if kind == 0:
            yp, sp = sb_prompt(hp, *mixer_params[i])
            ys, ss = sb_sample(hs, *layer_caches[i], page_table, *mixer_params[i])
        elif kind == 1:
            yp, sp = pool_prompt(hp, *mixer_params[i])
            ys, ss = pool_sample(hs, *layer_caches[i], *mixer_params[i])
        else:
            yp, sp = dsa_prompt(hp, *mixer_params[i])
            ys, ss = dsa_sample(hs, *layer_caches[i], page_table, *mixer_params[i])
        new_prompt.append(sp)
        new_sample.append(ss)
        xp = xp + yp
        xs = xs + ys
        f_norm, w_gate, w_up, w_down = ffn_params[i]
        xp = xp + swiglu(rmsnorm(xp, f_norm), w_gate, w_up, w_down)
        xs = xs + swiglu(rmsnorm(xs, f_norm), w_gate, w_up, w_down)
    k0_p, v0_p = new_prompt[0]
    k0_s, v0_s = new_sample[0]
    (pool1_p,) = new_prompt[1]
    (pool1_s,) = new_sample[1]
    k2_p, v2_p, kidx2_p = new_prompt[2]
    k2_s, v2_s, kidx2_s = new_sample[2]
    k3_p, v3_p = new_prompt[3]
    k3_s, v3_s = new_sample[3]
    return (xp, xs, k0_p, v0_p, k0_s, v0_s, pool1_p, pool1_s, k2_p, v2_p, kidx2_p, k2_s, v2_s, kidx2_s, k3_p, v3_p, k3_s, v3_s)
```

```python
import functools
import math

import jax
import jax.numpy as jnp
from jax import lax
from jax.experimental import pallas as pl
from jax.experimental.pallas import tpu as pltpu

F32 = jnp.float32
BF16 = jnp.bfloat16
I32 = jnp.int32

D_MODEL = 2048
N_HEADS = 16
HEAD_DIM = 128
ROT_DIM = 32
ROPE_THETA = 500000.0
POOL_WINDOWS = (2, 4, 8, 16)
POOL_GROUP_DIM = 512
POOL_STATE = 15
N_IDX_HEADS = 16
IDX_DIM = 64
IDX_ROT_DIM = 16
TOPK_MAX = 256
PAGE_SIZE = 128
PAST_LEN = 16384
RMS_EPS = 1e-6
ATTN_SCALE = HEAD_DIM ** -0.5
IDX_SCALE = IDX_DIM ** -0.5
IDX_W_SCALE = N_IDX_HEADS ** -0.5

LANES = 128
POOL_HALO = 16
VMEM_LIMIT_BYTES = 56 * 1024 * 1024
NEG_BIG = -1e30
INT_MIN = -2 ** 31


def _cparams(sem):
    return pltpu.CompilerParams(dimension_semantics=sem, vmem_limit_bytes=VMEM_LIMIT_BYTES)


def _pick(n, cands):
    for c in cands:
        if n % c == 0:
            return c
    return n


def _rmsnorm_kernel(x_ref, g_ref, o_ref):
    x = x_ref[...]
    ms = jnp.mean(x * x, axis=-1, keepdims=True)
    o_ref[...] = (x * lax.rsqrt(ms + RMS_EPS) * g_ref[...]).astype(o_ref.dtype)


def rmsnorm_rows(x, g, out_dtype=BF16):
    m, d = x.shape
    bm = _pick(m, (512, 256, 128, 64, 32, 16, 8))
    return pl.pallas_call(
        _rmsnorm_kernel,
        grid=(m // bm,),
        in_specs=[pl.BlockSpec((bm, d), lambda i: (i, 0)),
                  pl.BlockSpec((1, d), lambda i: (0, 0))],
        out_specs=pl.BlockSpec((bm, d), lambda i: (i, 0)),
        out_shape=jax.ShapeDtypeStruct((m, d), out_dtype),
        compiler_params=_cparams(("parallel",)),
        name="rmsnorm",
    )(x, g.reshape(1, d))


def _proj_kernel(*refs, norm, rope_shift, scale, n_out):
    x_ref, w_ref = refs[0], refs[1]
    pos = 2
    y = jnp.dot(x_ref[...], w_ref[...], preferred_element_type=F32)
    bn = y.shape[1]
    if norm:
        g_ref = refs[pos]
        pos += 1
        parts = []
        for h in range(bn // HEAD_DIM):
            yh = y[:, h * HEAD_DIM:(h + 1) * HEAD_DIM]
            ms = jnp.mean(yh * yh, axis=-1, keepdims=True)
            parts.append(yh * lax.rsqrt(ms + RMS_EPS))
        y = jnp.concatenate(parts, axis=-1) * g_ref[...]
    if rope_shift:
        c_ref, sa_ref, sb_ref = refs[pos:pos + 3]
        pos += 3
        reps = bn // LANES
        c = jnp.concatenate([c_ref[...]] * reps, axis=-1)
        sa = jnp.concatenate([sa_ref[...]] * reps, axis=-1)
        sb = jnp.concatenate([sb_ref[...]] * reps, axis=-1)
        y = y * c + pltpu.roll(y, rope_shift, 1) * sa + pltpu.roll(y, bn - rope_shift, 1) * sb
    if scale is not None:
        y = y * scale
    for o_ref in refs[pos:pos + n_out]:
        o_ref[...] = y.astype(o_ref.dtype)


def project(h, w, col0, n, out_dtypes, *, gain=None, rope=None, rope_shift=0, scale=None, bn=512):
    m, k = h.shape
    bm = _pick(m, (1024, 512, 256, 128, 64, 32, 16))
    bn = min(bn, n)
    assert n % bn == 0 and col0 % bn == 0 and bn % LANES == 0
    off = col0 // bn
    in_specs = [pl.BlockSpec((bm, k), lambda i, j: (i, 0)),
                pl.BlockSpec((k, bn), lambda i, j: (0, j + off))]
    args = [h, w]
    if gain is not None:
        in_specs.append(pl.BlockSpec((1, bn), lambda i, j: (0, 0)))
        args.append(jnp.tile(gain.reshape(1, -1), (1, bn // gain.shape[-1])))
    if rope is not None:
        period = rope[0].shape[0] // bm
        for tab in rope:
            in_specs.append(pl.BlockSpec((bm, LANES), lambda i, j: (i % period, 0)))
            args.append(tab)
    outs = pl.pallas_call(
        functools.partial(_proj_kernel, norm=gain is not None, rope_shift=rope_shift if rope is not None else 0,
                          scale=scale, n_out=len(out_dtypes)),
        grid=(m // bm, n // bn),
        in_specs=in_specs,
        out_specs=[pl.BlockSpec((bm, bn), lambda i, j: (i, j)) for _ in out_dtypes],
        out_shape=[jax.ShapeDtypeStruct((m, n), dt) for dt in out_dtypes],
        compiler_params=_cparams(("parallel", "parallel")),
        name="project",
    )(*args)
    return outs


def rope_tables(pos, rot_dim, head_dim):
    half = rot_dim // 2
    inv_freq = ROPE_THETA ** (-jnp.arange(half, dtype=F32) / half)
    ang = pos.astype(F32)[:, None] * inv_freq[None, :]
    cos, sin = jnp.cos(ang), jnp.sin(ang)
    p = pos.shape[0]
    ones = jnp.ones((p, head_dim - rot_dim), F32)
    zeros_h = jnp.zeros((p, half), F32)
    zeros_r = jnp.zeros((p, head_dim - rot_dim), F32)
    c = jnp.concatenate([cos, cos, ones], axis=-1)
    sa = jnp.concatenate([zeros_h, sin, zeros_r], axis=-1)
    sb = jnp.concatenate([-sin, zeros_h, zeros_r], axis=-1)
    reps = LANES // head_dim
    return tuple(jnp.tile(t, (1, reps)) for t in (c, sa, sb))


def _mm_res_kernel(a_ref, w_ref, r_ref, o_ref, acc_ref):
    kk = pl.program_id(2)

    @pl.when(kk == 0)
    def _():
        acc_ref[...] = r_ref[...]

    acc_ref[...] += jnp.dot(a_ref[...], w_ref[...], preferred_element_type=F32)

    @pl.when(kk == pl.num_programs(2) - 1)
    def _():
        o_ref[...] = acc_ref[...]


def matmul_residual(a, w, res):
    m, k = a.shape
    n = w.shape[1]
    bm = _pick(m, (1024, 512, 256, 128, 64, 32, 16))
    bn = _pick(n, (1024, 512, 256, 128))
    bk = _pick(k, (2048, 2816, 1024, 512))
    return pl.pallas_call(
        _mm_res_kernel,
        grid=(m // bm, n // bn, k // bk),
        in_specs=[pl.BlockSpec((bm, bk), lambda i, j, kk: (i, kk)),
                  pl.BlockSpec((bk, bn), lambda i, j, kk: (kk, j)),
                  pl.BlockSpec((bm, bn), lambda i, j, kk: (i, j))],
        out_specs=pl.BlockSpec((bm, bn), lambda i, j, kk: (i, j)),
        out_shape=jax.ShapeDtypeStruct((m, n), F32),
        scratch_shapes=[pltpu.VMEM((bm, bn), F32)],
        compiler_params=_cparams(("parallel", "parallel", "arbitrary")),
        name="matmul_residual",
    )(a, w, res)


def _ffn_up_kernel(x_ref, wg_ref, wu_ref, o_ref):
    x = x_ref[...]
    g = jnp.dot(x, wg_ref[...], preferred_element_type=F32)
    u = jnp.dot(x, wu_ref[...], preferred_element_type=F32)
    o_ref[...] = (g * (1.0 / (1.0 + jnp.exp(-g))) * u).astype(o_ref.dtype)


def ffn_up(h, wg, wu):
    m, k = h.shape
    n = wg.shape[1]
    bm = _pick(m, (1024, 512, 256, 128, 64, 32, 16))
    bn = _pick(n, (512, 256, 128))
    return pl.pallas_call(
        _ffn_up_kernel,
        grid=(m // bm, n // bn),
        in_specs=[pl.BlockSpec((bm, k), lambda i, j: (i, 0)),
                  pl.BlockSpec((k, bn), lambda i, j: (0, j)),
                  pl.BlockSpec((k, bn), lambda i, j: (0, j))],
        out_specs=pl.BlockSpec((bm, bn), lambda i, j: (i, j)),
        out_shape=jax.ShapeDtypeStruct((m, n), BF16),
        compiler_params=_cparams(("parallel", "parallel")),
        name="ffn_up",
    )(h, wg, wu)


def _sb_block(q, k, v, u, carry, acc, vis):
    z = lax.dot_general(q, k, (((1,), (1,)), ((), ())), preferred_element_type=F32) * ATTN_SCALE
    log_beta = jnp.minimum(z, 0.0) - jnp.log1p(jnp.exp(-jnp.abs(z)))
    log_keep = log_beta - z
    if vis is not None:
        log_keep = jnp.where(vis, log_keep, 0.0)
    hi = log_keep.astype(BF16)
    lo = (log_keep - hi.astype(F32)).astype(BF16)
    tail = (jnp.dot(hi, u, preferred_element_type=F32) + jnp.dot(lo, u, preferred_element_type=F32)) + carry
    w = jnp.exp(log_beta + tail)
    if vis is not None:
        w = jnp.where(vis, w, 0.0)
    acc = acc + jnp.dot(w.astype(BF16), v, preferred_element_type=F32)
    carry = carry + jnp.sum(log_keep, axis=-1, keepdims=True)
    return carry, acc


def _suffix_matrix(c):
    row = lax.broadcasted_iota(I32, (c, c), 0)
    col = lax.broadcasted_iota(I32, (c, c), 1)
    return (row > col).astype(BF16)


def _sb_prompt_kernel(q_ref, k_ref, v_ref, o_ref, *, bq):
    qi = pl.program_id(2)
    q = q_ref[...]
    u = _suffix_matrix(bq)
    row = lax.broadcasted_iota(I32, (bq, bq), 0)
    col = lax.broadcasted_iota(I32, (bq, bq), 1)
    start = pl.multiple_of(qi * bq, bq)
    carry = jnp.zeros((bq, 1), F32)
    acc = jnp.zeros((bq, HEAD_DIM), F32)
    carry, acc = _sb_block(q, k_ref[pl.ds(start, bq), :], v_ref[pl.ds(start, bq), :], u, carry, acc, col < row)

    def body(i, c):
        s0 = pl.multiple_of((qi - 1 - i) * bq, bq)
        return _sb_block(q, k_ref[pl.ds(s0, bq), :], v_ref[pl.ds(s0, bq), :], u, c[0], c[1], None)

    carry, acc = lax.fori_loop(0, qi, body, (carry, acc))
    o_ref[...] = acc.astype(o_ref.dtype)


def sb_attention_prompt(q, k, v, batch, t):
    bq = _pick(t, (256, 128))
    nq = t // bq
    return pl.pallas_call(
        functools.partial(_sb_prompt_kernel, bq=bq),
        grid=(batch, N_HEADS, nq),
        in_specs=[pl.BlockSpec((bq, HEAD_DIM), lambda b, h, i: (b * nq + i, h)),
                  pl.BlockSpec((t, HEAD_DIM), lambda b, h, i: (b, h)),
                  pl.BlockSpec((t, HEAD_DIM), lambda b, h, i: (b, h))],
        out_specs=pl.BlockSpec((bq, HEAD_DIM), lambda b, h, i: (b * nq + i, h)),
        out_shape=jax.ShapeDtypeStruct((batch * t, D_MODEL), BF16),
        compiler_params=_cparams(("parallel", "parallel", "arbitrary")),
        name="sb_attention_prompt",
    )(q, k, v)


def _block_diag_queries(q, db, t):
    q5 = q.reshape(db, t, N_HEADS, 1, HEAD_DIM)
    eye = jnp.eye(N_HEADS, dtype=jnp.bool_).reshape(1, 1, N_HEADS, N_HEADS, 1)
    return jnp.where(eye, q5, jnp.zeros_like(q5)).reshape(db, t * N_HEADS, D_MODEL)


def _extract_block_diag(acc, t):
    r = lax.broadcasted_iota(I32, acc.shape, 0)
    c = lax.broadcasted_iota(I32, acc.shape, 1)
    keep = (r % N_HEADS) == (c // HEAD_DIM)
    return jnp.sum(jnp.where(keep, acc, 0.0).reshape(t, N_HEADS, acc.shape[1]), axis=1)


def _load_page_by_head(page_ref):
    return jnp.concatenate([page_ref[pl.ds(h, PAGE_SIZE, stride=N_HEADS), :] for h in range(N_HEADS)],
                           axis=-1).astype(BF16)


def _sb_sample_kernel(pt_ref, q_ref, kc_ref, vc_ref, kn_ref, vn_ref, o_ref, carry_ref, acc_ref, *, t):
    j = pl.program_id(1)
    rows = t * N_HEADS
    u = _suffix_matrix(PAGE_SIZE)
    q = q_ref[...]

    @pl.when(j == 0)
    def _():
        r = lax.broadcasted_iota(I32, (rows, PAGE_SIZE), 0)
        c = lax.broadcasted_iota(I32, (rows, PAGE_SIZE), 1)
        carry, acc = _sb_block(q, kn_ref[...].astype(BF16), vn_ref[...].astype(BF16), u,
                               jnp.zeros((rows, 1), F32), jnp.zeros((rows, D_MODEL), F32), c < r // N_HEADS)
        carry_ref[...] = carry
        acc_ref[...] = acc

    @pl.when(j > 0)
    def _():
        carry, acc = _sb_block(q, _load_page_by_head(kc_ref), _load_page_by_head(vc_ref), u,
                               carry_ref[...], acc_ref[...], None)
        carry_ref[...] = carry
        acc_ref[...] = acc

    @pl.when(j == pl.num_programs(1) - 1)
    def _():
        o_ref[...] = _extract_block_diag(acc_ref[...], t)


def _pad_new_page(x, db, t):
    return jnp.pad(x.reshape(db, t, D_MODEL), ((0, 0), (0, PAGE_SIZE - t), (0, 0)))


def sb_attention_sample(q, k_new, v_new, cache_k, cache_v, page_table, db, t):
    n_pages = page_table.shape[1]
    rows = t * N_HEADS
    qbd = _block_diag_queries(q, db, t)

    def cache_map(b, j, pt):
        return (pt[b, n_pages - jnp.maximum(j, 1)], 0, 0)

    grid_spec = pltpu.PrefetchScalarGridSpec(
        num_scalar_prefetch=1,
        grid=(db, n_pages + 1),
        in_specs=[pl.BlockSpec((None, rows, D_MODEL), lambda b, j, pt: (b, 0, 0)),
                  pl.BlockSpec((None, PAGE_SIZE * N_HEADS, HEAD_DIM), cache_map),
                  pl.BlockSpec((None, PAGE_SIZE * N_HEADS, HEAD_DIM), cache_map),
                  pl.BlockSpec((None, PAGE_SIZE, D_MODEL), lambda b, j, pt: (b, 0, 0)),
                  pl.BlockSpec((None, PAGE_SIZE, D_MODEL), lambda b, j, pt: (b, 0, 0))],
        out_specs=pl.BlockSpec((None, t, D_MODEL), lambda b, j, pt: (b, 0, 0)),
        scratch_shapes=[pltpu.VMEM((rows, 1), F32), pltpu.VMEM((rows, D_MODEL), F32)],
    )
    out = pl.pallas_call(
        functools.partial(_sb_sample_kernel, t=t),
        grid_spec=grid_spec,
        out_shape=jax.ShapeDtypeStruct((db, t, D_MODEL), F32),
        compiler_params=_cparams(("parallel", "arbitrary")),
        name="sb_attention_sample",
    )(page_table, qbd, cache_k, cache_v, _pad_new_page(k_new, db, t), _pad_new_page(v_new, db, t))
    return out.reshape(db * t, D_MODEL)


def _pool_kernel(x_ref, halo_ref, g_ref, w_ref, ps_ref, o_ref, ext_ref, *, bm, blocks_per_seq, halo_is_state, pos0):
    i = pl.program_id(0)
    g = g_ref[...]

    def norm(x):
        ms = jnp.mean(x * x, axis=-1, keepdims=True)
        return x * lax.rsqrt(ms + RMS_EPS) * g

    x = x_ref[...]
    h = norm(x)
    if halo_is_state:
        halo = halo_ref[...]
    else:
        halo = jnp.where(i % blocks_per_seq == 0, 0.0, norm(halo_ref[...]))
    ext_ref[0:POOL_HALO, :] = halo
    ext_ref[POOL_HALO:POOL_HALO + bm, :] = h
    t_in_seq = (i % blocks_per_seq) * bm + lax.broadcasted_iota(I32, (bm, 1), 0) + pos0
    for gi, win in enumerate(POOL_WINDOWS):
        c0 = gi * POOL_GROUP_DIM
        c1 = c0 + POOL_GROUP_DIM
        wsum = h[:, c0:c1]
        for d in range(1, win):
            wsum = wsum + ext_ref[POOL_HALO - d:POOL_HALO - d + bm, c0:c1]
        count = jnp.minimum(t_in_seq + 1, win).astype(F32)
        pooled = wsum / count - h[:, c0:c1]
        y = jnp.dot(pooled.astype(BF16), w_ref[gi], preferred_element_type=F32)
        o_ref[:, c0:c1] = x[:, c0:c1] + y * ps_ref[:, c0:c1]


def pool_layer(x, halo_src, gain, w_pool, pool_scale, *, seq_len, halo_is_state, pos0):
    m, d = x.shape
    bm = _pick(seq_len, (256, 128, 64, 32, 16, 8))
    blocks_per_seq = seq_len // bm
    if halo_is_state:
        halo_spec = pl.BlockSpec((POOL_HALO, d), lambda i: (i, 0))
    else:
        per = bm // POOL_HALO
        halo_spec = pl.BlockSpec((POOL_HALO, d), lambda i: (jnp.maximum(i * per - 1, 0), 0))
    return pl.pallas_call(
        functools.partial(_pool_kernel, bm=bm, blocks_per_seq=blocks_per_seq, halo_is_state=halo_is_state, pos0=pos0),
        grid=(m // bm,),
        in_specs=[pl.BlockSpec((bm, d), lambda i: (i, 0)),
                  halo_spec,
                  pl.BlockSpec((1, d), lambda i: (0, 0)),
                  pl.BlockSpec(w_pool.shape, lambda i: (0, 0, 0)),
                  pl.BlockSpec((1, d), lambda i: (0, 0))],
        out_specs=pl.BlockSpec((bm, d), lambda i: (i, 0)),
        out_shape=jax.ShapeDtypeStruct((m, d), F32),
        scratch_shapes=[pltpu.VMEM((POOL_HALO + bm, d), F32)],
        compiler_params=_cparams(("parallel",)),
        name="pool_layer",
    )(x, halo_src, gain.reshape(1, d), w_pool, pool_scale.reshape(1, d))


def _sortable_key(x):
    bits = pltpu.bitcast(x, I32)
    return jnp.where(bits < 0, bits ^ jnp.int32(0x7FFFFFFF), bits)


def _lane_chunks(x):
    return [x[:, c * LANES:(c + 1) * LANES] for c in range(x.shape[1] // LANES)]


def _topk_bias(keys_ref, bias_ref, tie_ref, n_blocks, blk, q_pos, topk):
    rows = q_pos.shape[0]

    def count(pred):
        def body(kb, cnt):
            s0 = pl.multiple_of(kb * blk, blk)
            keys = keys_ref[:, pl.ds(s0, blk)]
            for c, kc in enumerate(_lane_chunks(keys)):
                pos = s0 + c * LANES + lax.broadcasted_iota(I32, (rows, LANES), 1)
                cnt = cnt + jnp.where(pred(kc, pos), 1.0, 0.0)
            return cnt
        cnt = lax.fori_loop(0, n_blocks, body, jnp.zeros((rows, LANES), F32))
        return jnp.sum(cnt, axis=-1, keepdims=True)

    def count_ge(cand):
        return count(lambda kc, pos: kc >= cand)

    thr = jnp.full((rows, 1), INT_MIN, I32)
    thr = jnp.where(count_ge(jnp.zeros((rows, 1), I32)) >= topk, 0, thr)

    def bit_body(i, thr):
        cand = thr | lax.shift_left(jnp.int32(1), 30 - i)
        return jnp.where(count_ge(cand) >= topk, cand, thr)

    thr = lax.fori_loop(0, 31, bit_body, thr)
    n_ge = count_ge(thr)
    n_gt = count(lambda kc, pos: kc > thr)
    need = topk - n_gt
    tie_ref[...] = jnp.full((rows, 1), 2 ** 30, I32)

    @pl.when(jnp.max(n_ge) > topk)
    def _():
        n_bits = max(1, int(math.ceil(math.log2(keys_ref.shape[1] + 1))))

        def cut_body(i, lo):
            cand = lo + lax.shift_left(jnp.int32(1), n_bits - 1 - i)
            f = count(lambda kc, pos: (kc == thr) & (pos <= cand))
            return jnp.where(f < need, cand, lo)

        lo = lax.fori_loop(0, n_bits, cut_body, jnp.full((rows, 1), -1, I32))
        tie_ref[...] = lo + 1

    cut = tie_ref[...]

    def write_body(kb, carry):
        s0 = pl.multiple_of(kb * blk, blk)
        keys = keys_ref[:, pl.ds(s0, blk)]
        pos = s0 + lax.broadcasted_iota(I32, (rows, blk), 1)
        sel = ((keys > thr) | ((keys == thr) & (pos <= cut))) & (pos <= q_pos)
        bias_ref[:, pl.ds(s0, blk)] = jnp.where(sel, 0.0, NEG_BIG)
        return carry

    lax.fori_loop(0, n_blocks, write_body, 0)


def _online_softmax_block(q, k, v, bias, m, l, acc):
    s = lax.dot_general(q, k, (((1,), (1,)), ((), ())), preferred_element_type=F32) * ATTN_SCALE + bias
    m_new = jnp.maximum(m, jnp.max(s, axis=-1, keepdims=True))
    a = jnp.exp(m - m_new)
    p = jnp.exp(s - m_new)
    l = a * l + jnp.sum(p, axis=-1, keepdims=True)
    acc = a * acc + jnp.dot(p.astype(BF16), v, preferred_element_type=F32)
    return m_new, l, acc


def _dsa_prompt_kernel(q_ref, k_ref, v_ref, iq_ref, ik_ref, iw_ref, o_ref, keys_ref, bias_ref, tie_ref, *, bq, topk):
    qi = pl.program_id(1)
    h = pl.program_id(2)
    n_blocks = qi + 1
    q_pos = qi * bq + lax.broadcasted_iota(I32, (bq, 1), 0)

    @pl.when(h == 0)
    def _():
        iw = iw_ref[...]
        iq_heads = [iq_ref[:, hh * IDX_DIM:(hh + 1) * IDX_DIM] for hh in range(N_IDX_HEADS)]
        iw_cols = [iw[:, hh:hh + 1] for hh in range(N_IDX_HEADS)]

        def score_body(kb, carry):
            s0 = pl.multiple_of(kb * bq, bq)
            ik = ik_ref[pl.ds(s0, bq), :]
            sc = jnp.zeros((bq, bq), F32)
            for hh in range(N_IDX_HEADS):
                d = lax.dot_general(iq_heads[hh], ik, (((1,), (1,)), ((), ())), preferred_element_type=F32)
                sc = sc + jnp.maximum(d, 0.0) * iw_cols[hh]
            pos = s0 + lax.broadcasted_iota(I32, (bq, bq), 1)
            sc = jnp.where(pos <= q_pos, sc, -jnp.inf)
            keys_ref[:, pl.ds(s0, bq)] = _sortable_key(sc)
            return carry

        lax.fori_loop(0, n_blocks, score_body, 0)
        _topk_bias(keys_ref, bias_ref, tie_ref, n_blocks, bq, q_pos, topk)

    q = q_ref[...]

    def body(kb, c):
        s0 = pl.multiple_of(kb * bq, bq)
        return _online_softmax_block(q, k_ref[pl.ds(s0, bq), :], v_ref[pl.ds(s0, bq), :],
                                     bias_ref[:, pl.ds(s0, bq)], *c)

    m0 = jnp.full((bq, 1), NEG_BIG, F32)
    m, l, acc = lax.fori_loop(0, n_blocks, body, (m0, jnp.zeros((bq, 1), F32), jnp.zeros((bq, HEAD_DIM), F32)))
    o_ref[...] = (acc / l).astype(o_ref.dtype)


def dsa_attention_prompt(q, k, v, iq, ik, iw, batch, t):
    bq = _pick(t, (256, 128))
    nq = t // bq
    topk = min(TOPK_MAX, t // 4)
    n_iq = N_IDX_HEADS * IDX_DIM
    return pl.pallas_call(
        functools.partial(_dsa_prompt_kernel, bq=bq, topk=topk),
        grid=(batch, nq, N_HEADS),
        in_specs=[pl.BlockSpec((bq, HEAD_DIM), lambda b, i, h: (b * nq + i, h)),
                  pl.BlockSpec((t, HEAD_DIM), lambda b, i, h: (b, h)),
                  pl.BlockSpec((t, HEAD_DIM), lambda b, i, h: (b, h)),
                  pl.BlockSpec((bq, n_iq), lambda b, i, h: (b * nq + i, 0)),
                  pl.BlockSpec((t, IDX_DIM), lambda b, i, h: (b, 0)),
                  pl.BlockSpec((bq, N_IDX_HEADS), lambda b, i, h: (b * nq + i, 0))],
        out_specs=pl.BlockSpec((bq, HEAD_DIM), lambda b, i, h: (b * nq + i, h)),
        out_shape=jax.ShapeDtypeStruct((batch * t, D_MODEL), BF16),
        scratch_shapes=[pltpu.VMEM((bq, t), I32), pltpu.VMEM((bq, t), F32), pltpu.VMEM((bq, 1), I32)],
        compiler_params=_cparams(("parallel", "arbitrary", "arbitrary")),
        name="dsa_attention_prompt",
    )(q, k, v, iq, ik, iw)


def _idx_score_sample_kernel(pt_ref, iq_ref, iw_ref, ikc_ref, ikn_ref, o_ref, *, t, n_pages):
    j = pl.program_id(1)
    rows = t * N_IDX_HEADS

    def scores(ik):
        d = lax.dot_general(iq_ref[...], ik.astype(BF16), (((1,), (1,)), ((), ())), preferred_element_type=F32)
        wd = jnp.maximum(d, 0.0) * iw_ref[...]
        return jnp.sum(wd.reshape(t, N_IDX_HEADS, PAGE_SIZE), axis=1)

    @pl.when(j < n_pages)
    def _():
        o_ref[...] = scores(ikc_ref[...])

    @pl.when(j == n_pages)
    def _():
        o_ref[...] = scores(ikn_ref[...])


def idx_scores_sample(iq, iw, ik_new, cache_kidx, page_table, db, t):
    n_pages = page_table.shape[1]
    rows = t * N_IDX_HEADS
    iq_rows = iq.reshape(db, rows, IDX_DIM)
    iw_rows = iw.reshape(db, rows, 1)
    ikn = jnp.pad(ik_new.reshape(db, t, IDX_DIM), ((0, 0), (0, PAGE_SIZE - t), (0, 0)))
    grid_spec = pltpu.PrefetchScalarGridSpec(
        num_scalar_prefetch=1,
        grid=(db, n_pages + 1),
        in_specs=[pl.BlockSpec((None, rows, IDX_DIM), lambda b, j, pt: (b, 0, 0)),
                  pl.BlockSpec((None, rows, 1), lambda b, j, pt: (b, 0, 0)),
                  pl.BlockSpec((None, PAGE_SIZE, IDX_DIM), lambda b, j, pt: (pt[b, jnp.minimum(j, n_pages - 1)], 0, 0)),
                  pl.BlockSpec((None, PAGE_SIZE, IDX_DIM), lambda b, j, pt: (b, 0, 0))],
        out_specs=pl.BlockSpec((None, t, PAGE_SIZE), lambda b, j, pt: (b, 0, j)),
    )
    return pl.pallas_call(
        functools.partial(_idx_score_sample_kernel, t=t, n_pages=n_pages),
        grid_spec=grid_spec,
        out_shape=jax.ShapeDtypeStruct((db, t, (n_pages + 1) * PAGE_SIZE), F32),
        compiler_params=_cparams(("parallel", "arbitrary")),
        name="idx_scores_sample",
    )(page_table, iq_rows, iw_rows, cache_kidx, ikn)


def _topk_sample_kernel(s_ref, qpos_ref, bias_ref, keys_ref, tie_ref, *, topk, n_blocks):
    q_pos = qpos_ref[...]
    rows = q_pos.shape[0]

    def fill(kb, carry):
        s0 = pl.multiple_of(kb * LANES, LANES)
        pos = s0 + lax.broadcasted_iota(I32, (rows, LANES), 1)
        sc = jnp.where(pos <= q_pos, s_ref[:, pl.ds(s0, LANES)], -jnp.inf)
        keys_ref[:, pl.ds(s0, LANES)] = _sortable_key(sc)
        return carry

    lax.fori_loop(0, n_blocks, fill, 0)
    _topk_bias(keys_ref, bias_ref, tie_ref, n_blocks, LANES, q_pos, topk)


def topk_bias_sample(scores, q_pos, topk):
    r, l = scores.shape
    return pl.pallas_call(
        functools.partial(_topk_sample_kernel, topk=topk, n_blocks=l // LANES),
        out_shape=jax.ShapeDtypeStruct((r, l), F32),
        scratch_shapes=[pltpu.VMEM((r, l), I32), pltpu.VMEM((r, 1), I32)],
        compiler_params=pltpu.CompilerParams(vmem_limit_bytes=VMEM_LIMIT_BYTES),
        name="topk_bias_sample",
    )(scores, q_pos)


def _dsa_sample_kernel(pt_ref, q_ref, bias_ref, kc_ref, vc_ref, kn_ref, vn_ref, o_ref, m_ref, l_ref, acc_ref,
                       *, t, n_pages):
    j = pl.program_id(1)
    rows = t * N_HEADS

    @pl.when(j == 0)
    def _():
        m_ref[...] = jnp.full((rows, 1), NEG_BIG, F32)
        l_ref[...] = jnp.zeros((rows, 1), F32)
        acc_ref[...] = jnp.zeros((rows, D_MODEL), F32)

    bias = jnp.broadcast_to(bias_ref[...][:, None, :], (t, N_HEADS, PAGE_SIZE)).reshape(rows, PAGE_SIZE)

    def step(k, v):
        m, l, acc = _online_softmax_block(q_ref[...], k, v, bias, m_ref[...], l_ref[...], acc_ref[...])
        m_ref[...] = m
        l_ref[...] = l
        acc_ref[...] = acc

    @pl.when(j < n_pages)
    def _():
        step(_load_page_by_head(kc_ref), _load_page_by_head(vc_ref))

    @pl.when(j == n_pages)
    def _():
        step(kn_ref[...].astype(BF16), vn_ref[...].astype(BF16))
        o_ref[...] = _extract_block_diag(acc_ref[...] / l_ref[...], t)


def dsa_attention_sample(q, bias, k_new, v_new, cache_k, cache_v, page_table, db, t):
    n_pages = page_table.shape[1]
    rows = t * N_HEADS
    qbd = _block_diag_queries(q, db, t)

    def cache_map(b, j, pt):
        return (pt[b, jnp.minimum(j, n_pages - 1)], 0, 0)

    grid_spec = pltpu.PrefetchScalarGridSpec(
        num_scalar_prefetch=1,
        grid=(db, n_pages + 1),
        in_specs=[pl.BlockSpec((None, rows, D_MODEL), lambda b, j, pt: (b, 0, 0)),
                  pl.BlockSpec((None, t, PAGE_SIZE), lambda b, j, pt: (b, 0, j)),
                  pl.BlockSpec((None, PAGE_SIZE * N_HEADS, HEAD_DIM), cache_map),
                  pl.BlockSpec((None, PAGE_SIZE * N_HEADS, HEAD_DIM), cache_map),
                  pl.BlockSpec((None, PAGE_SIZE, D_MODEL), lambda b, j, pt: (b, 0, 0)),
                  pl.BlockSpec((None, PAGE_SIZE, D_MODEL), lambda b, j, pt: (b, 0, 0))],
        out_specs=pl.BlockSpec((None, t, D_MODEL), lambda b, j, pt: (b, 0, 0)),
        scratch_shapes=[pltpu.VMEM((rows, 1), F32), pltpu.VMEM((rows, 1), F32), pltpu.VMEM((rows, D_MODEL), F32)],
    )
    out = pl.pallas_call(
        functools.partial(_dsa_sample_kernel, t=t, n_pages=n_pages),
        grid_spec=grid_spec,
        out_shape=jax.ShapeDtypeStruct((db, t, D_MODEL), F32),
        compiler_params=_cparams(("parallel", "arbitrary")),
        name="dsa_attention_sample",
    )(page_table, qbd, bias, cache_k, cache_v, _pad_new_page(k_new, db, t), _pad_new_page(v_new, db, t))
    return out.reshape(db * t, D_MODEL)


def _ffn(x, f_norm, wg, wu, wd):
    h = rmsnorm_rows(x, f_norm)
    return matmul_residual(ffn_up(h, wg, wu), wd, x)


def _sb_projections(x, attn_norm, w_qkv, q_norm, k_norm):
    h = rmsnorm_rows(x, attn_norm)
    (q,) = project(h, w_qkv, 0, D_MODEL, (BF16,), gain=q_norm)
    k32, k16 = project(h, w_qkv, D_MODEL, D_MODEL, (F32, BF16), gain=k_norm)
    v32, v16 = project(h, w_qkv, 2 * D_MODEL, D_MODEL, (F32, BF16))
    return q, k32, k16, v32, v16


def _dsa_projections(x, attn_norm, w_in, w_small, q_norm, k_norm, rope_qk, rope_idx):
    h = rmsnorm_rows(x, attn_norm)
    (q,) = project(h, w_in, 0, D_MODEL, (BF16,), gain=q_norm, rope=rope_qk, rope_shift=ROT_DIM // 2)
    k32, k16 = project(h, w_in, D_MODEL, D_MODEL, (F32, BF16), gain=k_norm, rope=rope_qk, rope_shift=ROT_DIM // 2)
    v32, v16 = project(h, w_in, 2 * D_MODEL, D_MODEL, (F32, BF16))
    (iq,) = project(h, w_in, 3 * D_MODEL, N_IDX_HEADS * IDX_DIM, (BF16,), rope=rope_idx, rope_shift=IDX_ROT_DIM // 2)
    (tail_rot,) = project(h, w_small, 0, LANES, (F32,), rope=rope_idx, rope_shift=IDX_ROT_DIM // 2)
    (tail_raw,) = project(h, w_small, 0, LANES, (F32,))
    ik32 = tail_rot[:, :IDX_DIM]
    iw = tail_raw[:, IDX_DIM:IDX_DIM + N_IDX_HEADS]
    return q, k32, k16, v32, v16, iq, ik32, iw


def kernel(x_prompt, x_sample, cache_k_0, cache_v_0, state_pool_1, cache_k_2, cache_v_2, cache_kidx_2, cache_k_3, cache_v_3, page_table, attn_norm_0, w_qkv_0, q_norm_0, k_norm_0, w_o_0, ffn_norm_0, w_gate_0, w_up_0, w_down_0, attn_norm_1, w_pool_1, pool_scale_1, ffn_norm_1, w_gate_1, w_up_1, w_down_1, attn_norm_2, w_in_2, q_norm_2, k_norm_2, w_o_2, ffn_norm_2, w_gate_2, w_up_2, w_down_2, attn_norm_3, w_qkv_3, q_norm_3, k_norm_3, w_o_3, ffn_norm_3, w_gate_3, w_up_3, w_down_3):
    batch, t, d = x_prompt.shape
    db, dt, _ = x_sample.shape
    n_phys = cache_k_0.shape[0]
    n_pages = page_table.shape[1]
    past = n_pages * PAGE_SIZE
    mp, ms = batch * t, db * dt
    xp = x_prompt.reshape(mp, d)
    xs = x_sample.reshape(ms, d)

    def flat_cache(c):
        return c.reshape(n_phys, PAGE_SIZE * N_HEADS, HEAD_DIM)

    def bf(w):
        return w.astype(BF16)

    def heads(a, b_, t_):
        return a.reshape(b_, t_, N_HEADS, HEAD_DIM)

    def sb_layer(xp, xs, cache_k, cache_v, attn_norm, w_qkv, q_norm, k_norm, w_o):
        w_qkv, w_o = bf(w_qkv), bf(w_o)
        q, k32, k16, v32, v16 = _sb_projections(xp, attn_norm, w_qkv, q_norm, k_norm)
        xp = matmul_residual(sb_attention_prompt(q, k16, v16, batch, t), w_o, xp)
        qs, ks32, _, vs32, _ = _sb_projections(xs, attn_norm, w_qkv, q_norm, k_norm)
        o_s = sb_attention_sample(qs, ks32, vs32, flat_cache(cache_k), flat_cache(cache_v), page_table, db, dt)
        xs = matmul_residual(o_s.astype(BF16), w_o, xs)
        return xp, xs, (heads(k32, batch, t), heads(v32, batch, t)), (heads(ks32, db, dt), heads(vs32, db, dt))

    xp, xs, (k0_p, v0_p), (k0_s, v0_s) = sb_layer(xp, xs, cache_k_0, cache_v_0, attn_norm_0, w_qkv_0, q_norm_0,
                                                   k_norm_0, w_o_0)
    xp = _ffn(xp, ffn_norm_0, bf(w_gate_0), bf(w_up_0), bf(w_down_0))
    xs = _ffn(xs, ffn_norm_0, bf(w_gate_0), bf(w_up_0), bf(w_down_0))

    w_pool = bf(w_pool_1)
    pool1_p = rmsnorm_rows(xp.reshape(batch, t, d)[:, t - POOL_HALO:].reshape(batch * POOL_HALO, d), attn_norm_1,
                           F32).reshape(batch, POOL_HALO, d)[:, POOL_HALO - POOL_STATE:]
    hs_new = rmsnorm_rows(xs, attn_norm_1, F32)
    pool1_s = jnp.concatenate([state_pool_1, hs_new.reshape(db, dt, d)], axis=1)[:, -POOL_STATE:]
    xp = pool_layer(xp, xp, attn_norm_1, w_pool, pool_scale_1, seq_len=t, halo_is_state=False, pos0=0)
    rows_s = 8
    xs_pad = jnp.pad(xs.reshape(db, dt, d), ((0, 0), (0, rows_s - dt), (0, 0))).reshape(db * rows_s, d)
    state_pad = jnp.pad(state_pool_1, ((0, 0), (POOL_HALO - POOL_STATE, 0), (0, 0))).reshape(db * POOL_HALO, d)
    xs = pool_layer(xs_pad, state_pad, attn_norm_1, w_pool, pool_scale_1, seq_len=rows_s, halo_is_state=True,
                    pos0=PAST_LEN).reshape(db, rows_s, d)[:, :dt].reshape(ms, d)
    xp = _ffn(xp, ffn_norm_1, bf(w_gate_1), bf(w_up_1), bf(w_down_1))
    xs = _ffn(xs, ffn_norm_1, bf(w_gate_1), bf(w_up_1), bf(w_down_1))

    w_in, w_o2 = bf(w_in_2), bf(w_o_2)
    n_main = 3 * D_MODEL + N_IDX_HEADS * IDX_DIM
    w_small = jnp.pad(w_in[:, n_main:], ((0, 0), (0, LANES - (w_in.shape[1] - n_main))))
    pos_p = jnp.arange(t, dtype=I32)
    pos_s = jnp.tile(past + jnp.arange(dt, dtype=I32), db)
    q, k32, k16, v32, v16, iq, ik32, iw = _dsa_projections(
        xp, attn_norm_2, w_in, w_small, q_norm_2, k_norm_2,
        rope_tables(pos_p, ROT_DIM, HEAD_DIM), rope_tables(pos_p, IDX_ROT_DIM, IDX_DIM))
    iw_scaled = iw * (IDX_W_SCALE * IDX_SCALE)
    o_p = dsa_attention_prompt(q, k16, v16, iq, ik32.astype(BF16), iw_scaled, batch, t)
    k2_p, v2_p, kidx2_p = heads(k32, batch, t), heads(v32, batch, t), ik32.reshape(batch, t, IDX_DIM)
    xp = matmul_residual(o_p, w_o2, xp)

    qs, ks32, _, vs32, _, iqs, iks32, iws = _dsa_projections(
        xs, attn_norm_2, w_in, w_small, q_norm_2, k_norm_2,
        rope_tables(pos_s, ROT_DIM, HEAD_DIM), rope_tables(pos_s, IDX_ROT_DIM, IDX_DIM))
    scores = idx_scores_sample(iqs, iws * (IDX_W_SCALE * IDX_SCALE), iks32, cache_kidx_2, page_table, db, dt)
    topk_s = min(TOPK_MAX, (past + dt) // 4)
    bias_s = topk_bias_sample(scores.reshape(ms, -1), pos_s.reshape(ms, 1), topk_s).reshape(db, dt, -1)
    o_s = dsa_attention_sample(qs, bias_s, ks32, vs32, flat_cache(cache_k_2), flat_cache(cache_v_2), page_table, db, dt)
    k2_s, v2_s, kidx2_s = heads(ks32, db, dt), heads(vs32, db, dt), iks32.reshape(db, dt, IDX_DIM)
    xs = matmul_residual(o_s.astype(BF16), w_o2, xs)
    xp = _ffn(xp, ffn_norm_2, bf(w_gate_2), bf(w_up_2), bf(w_down_2))
    xs = _ffn(xs, ffn_norm_2, bf(w_gate_2), bf(w_up_2), bf(w_down_2))

    xp, xs, (k3_p, v3_p), (k3_s, v3_s) = sb_layer(xp, xs, cache_k_3, cache_v_3, attn_norm_3, w_qkv_3, q_norm_3,
                                                   k_norm_3, w_o_3)
    xp = _ffn(xp, ffn_norm_3, bf(w_gate_3), bf(w_up_3), bf(w_down_3))
    xs = _ffn(xs, ffn_norm_3, bf(w_gate_3), bf(w_up_3), bf(w_down_3))

    return (xp.reshape(batch, t, d), xs.reshape(db, dt, d), k0_p, v0_p, k0_s, v0_s, pool1_p, pool1_s,
            k2_p, v2_p, kidx2_p, k2_s, v2_s, kidx2_s, k3_p, v3_p, k3_s, v3_s)
```

```python
import functools
import math

import jax
import jax.numpy as jnp
from jax import lax
from jax.experimental import pallas as pl
from jax.experimental.pallas import tpu as pltpu

F32 = jnp.float32
BF16 = jnp.bfloat16
I32 = jnp.int32

D_MODEL = 2048
N_HEADS = 16
HEAD_DIM = 128
ROT_DIM = 32
ROPE_THETA = 500000.0
POOL_WINDOWS = (2, 4, 8, 16)
POOL_GROUP_DIM = 512
POOL_STATE = 15
N_IDX_HEADS = 16
IDX_DIM = 64
IDX_ROT_DIM = 16
TOPK_MAX = 256
PAGE_SIZE = 128
PAST_LEN = 16384
RMS_EPS = 1e-6
ATTN_SCALE = HEAD_DIM ** -0.5
IDX_SCALE = IDX_DIM ** -0.5
IDX_W_SCALE = N_IDX_HEADS ** -0.5
LOG2E = 1.4426950408889634

HEADS_PER_STEP = 4
SAMPLE_PAGES_PER_STEP = 4
IDX_PAGES_PER_STEP = 16
HEAD_GROUP = 8

LANES = 128
POOL_HALO = 16
VMEM_LIMIT_BYTES = 56 * 1024 * 1024
NEG_BIG = -1e30
INT_MIN = -2 ** 31


def _cparams(sem):
    return pltpu.CompilerParams(dimension_semantics=sem, vmem_limit_bytes=VMEM_LIMIT_BYTES)


def _pick(n, cands):
    for c in cands:
        if n % c == 0:
            return c
    return n


def _rmsnorm_kernel(x_ref, g_ref, o_ref):
    x = x_ref[...]
    ms = jnp.mean(x * x, axis=-1, keepdims=True)
    o_ref[...] = (x * lax.rsqrt(ms + RMS_EPS) * g_ref[...]).astype(o_ref.dtype)


def rmsnorm_rows(x, g, out_dtype=BF16):
    m, d = x.shape
    bm = _pick(m, (512, 256, 128, 64, 32, 16, 8))
    return pl.pallas_call(
        _rmsnorm_kernel,
        grid=(m // bm,),
        in_specs=[pl.BlockSpec((bm, d), lambda i: (i, 0)),
                  pl.BlockSpec((1, d), lambda i: (0, 0))],
        out_specs=pl.BlockSpec((bm, d), lambda i: (i, 0)),
        out_shape=jax.ShapeDtypeStruct((m, d), out_dtype),
        compiler_params=_cparams(("parallel",)),
        name="rmsnorm",
    )(x, g.reshape(1, d))


def _proj_kernel(*refs, norm, rope_shift, scale, n_out):
    x_ref, w_ref = refs[0], refs[1]
    pos = 2
    y = jnp.dot(x_ref[...], w_ref[...], preferred_element_type=F32)
    bn = y.shape[1]
    if norm:
        g_ref = refs[pos]
        pos += 1
        parts = []
        for h in range(bn // HEAD_DIM):
            yh = y[:, h * HEAD_DIM:(h + 1) * HEAD_DIM]
            ms = jnp.mean(yh * yh, axis=-1, keepdims=True)
            parts.append(yh * lax.rsqrt(ms + RMS_EPS))
        y = jnp.concatenate(parts, axis=-1) * g_ref[...]
    if rope_shift:
        c_ref, sa_ref, sb_ref = refs[pos:pos + 3]
        pos += 3
        reps = bn // LANES
        c = jnp.concatenate([c_ref[...]] * reps, axis=-1)
        sa = jnp.concatenate([sa_ref[...]] * reps, axis=-1)
        sb = jnp.concatenate([sb_ref[...]] * reps, axis=-1)
        y = y * c + pltpu.roll(y, rope_shift, 1) * sa + pltpu.roll(y, bn - rope_shift, 1) * sb
    if scale is not None:
        y = y * scale
    for o_ref in refs[pos:pos + n_out]:
        o_ref[...] = y.astype(o_ref.dtype)


def project(h, w, col0, n, out_dtypes, *, gain=None, rope=None, rope_shift=0, scale=None, bn=512):
    m, k = h.shape
    bm = _pick(m, (1024, 512, 256, 128, 64, 32, 16))
    bn = min(bn, n)
    assert n % bn == 0 and col0 % bn == 0 and bn % LANES == 0
    off = col0 // bn
    in_specs = [pl.BlockSpec((bm, k), lambda i, j: (i, 0)),
                pl.BlockSpec((k, bn), lambda i, j: (0, j + off))]
    args = [h, w]
    if gain is not None:
        in_specs.append(pl.BlockSpec((1, bn), lambda i, j: (0, 0)))
        args.append(jnp.tile(gain.reshape(1, -1), (1, bn // gain.shape[-1])))
    if rope is not None:
        period = rope[0].shape[0] // bm
        for tab in rope:
            in_specs.append(pl.BlockSpec((bm, LANES), lambda i, j: (i % period, 0)))
            args.append(tab)
    outs = pl.pallas_call(
        functools.partial(_proj_kernel, norm=gain is not None, rope_shift=rope_shift if rope is not None else 0,
                          scale=scale, n_out=len(out_dtypes)),
        grid=(m // bm, n // bn),
        in_specs=in_specs,
        out_specs=[pl.BlockSpec((bm, bn), lambda i, j: (i, j)) for _ in out_dtypes],
        out_shape=[jax.ShapeDtypeStruct((m, n), dt) for dt in out_dtypes],
        compiler_params=_cparams(("parallel", "parallel")),
        name="project",
    )(*args)
    return outs


def rope_tables(pos, rot_dim, head_dim):
    half = rot_dim // 2
    inv_freq = ROPE_THETA ** (-jnp.arange(half, dtype=F32) / half)
    ang = pos.astype(F32)[:, None] * inv_freq[None, :]
    cos, sin = jnp.cos(ang), jnp.sin(ang)
    p = pos.shape[0]
    ones = jnp.ones((p, head_dim - rot_dim), F32)
    zeros_h = jnp.zeros((p, half), F32)
    zeros_r = jnp.zeros((p, head_dim - rot_dim), F32)
    c = jnp.concatenate([cos, cos, ones], axis=-1)
    sa = jnp.concatenate([zeros_h, sin, zeros_r], axis=-1)
    sb = jnp.concatenate([-sin, zeros_h, zeros_r], axis=-1)
    reps = LANES // head_dim
    return tuple(jnp.tile(t, (1, reps)) for t in (c, sa, sb))


def _mm_res_kernel(a_ref, w_ref, r_ref, o_ref, acc_ref):
    kk = pl.program_id(2)

    @pl.when(kk == 0)
    def _():
        acc_ref[...] = r_ref[...]

    acc_ref[...] += jnp.dot(a_ref[...], w_ref[...], preferred_element_type=F32)

    @pl.when(kk == pl.num_programs(2) - 1)
    def _():
        o_ref[...] = acc_ref[...]


def matmul_residual(a, w, res):
    m, k = a.shape
    n = w.shape[1]
    bm = _pick(m, (1024, 512, 256, 128, 64, 32, 16))
    bn = _pick(n, (1024, 512, 256, 128))
    bk = _pick(k, (2048, 2816, 1024, 512))
    return pl.pallas_call(
        _mm_res_kernel,
        grid=(m // bm, n // bn, k // bk),
        in_specs=[pl.BlockSpec((bm, bk), lambda i, j, kk: (i, kk)),
                  pl.BlockSpec((bk, bn), lambda i, j, kk: (kk, j)),
                  pl.BlockSpec((bm, bn), lambda i, j, kk: (i, j))],
        out_specs=pl.BlockSpec((bm, bn), lambda i, j, kk: (i, j)),
        out_shape=jax.ShapeDtypeStruct((m, n), F32),
        scratch_shapes=[pltpu.VMEM((bm, bn), F32)],
        compiler_params=_cparams(("parallel", "parallel", "arbitrary")),
        name="matmul_residual",
    )(a, w, res)


def _ffn_up_kernel(x_ref, wg_ref, wu_ref, o_ref):
    x = x_ref[...]
    g = jnp.dot(x, wg_ref[...], preferred_element_type=F32)
    u = jnp.dot(x, wu_ref[...], preferred_element_type=F32)
    o_ref[...] = (g * (1.0 / (1.0 + jnp.exp(-g))) * u).astype(o_ref.dtype)


def ffn_up(h, wg, wu):
    m, k = h.shape
    n = wg.shape[1]
    bm = _pick(m, (1024, 512, 256, 128, 64, 32, 16))
    bn = _pick(n, (512, 256, 128))
    return pl.pallas_call(
        _ffn_up_kernel,
        grid=(m // bm, n // bn),
        in_specs=[pl.BlockSpec((bm, k), lambda i, j: (i, 0)),
                  pl.BlockSpec((k, bn), lambda i, j: (0, j)),
                  pl.BlockSpec((k, bn), lambda i, j: (0, j))],
        out_specs=pl.BlockSpec((bm, bn), lambda i, j: (i, j)),
        out_shape=jax.ShapeDtypeStruct((m, n), BF16),
        compiler_params=_cparams(("parallel", "parallel")),
        name="ffn_up",
    )(h, wg, wu)


def _sb_blocks(qs, ks, vs, u2, state, vis):
    n = len(qs)
    zs = [lax.dot_general(qs[i], ks[i], (((1,), (1,)), ((), ())), preferred_element_type=F32) for i in range(n)]
    log_betas, sums, hls = [], [], []
    for z in zs:
        z = z * (ATTN_SCALE * LOG2E)
        neg_abs = pltpu.bitcast(pltpu.bitcast(z, I32) | jnp.int32(INT_MIN), F32)
        log_beta = jnp.minimum(z, 0.0) - jnp.log(1.0 + jnp.exp2(neg_abs)) * LOG2E
        log_keep = log_beta - z
        if vis is not None:
            log_keep = jnp.where(vis, log_keep, 0.0)
        hi = log_keep.astype(BF16)
        lo = (log_keep - hi.astype(F32)).astype(BF16)
        log_betas.append(log_beta)
        sums.append(jnp.sum(log_keep, axis=-1, keepdims=True))
        hls.append(jnp.concatenate([hi, lo], axis=-1))
    tails = [jnp.dot(hl, u2, preferred_element_type=F32) for hl in hls]
    ws = []
    for i in range(n):
        w = jnp.exp2(log_betas[i] + (tails[i] + state[i][0]))
        if vis is not None:
            w = jnp.where(vis, w, 0.0)
        ws.append(w.astype(BF16))
    return tuple((state[i][0] + sums[i], state[i][1] + jnp.dot(ws[i], vs[i], preferred_element_type=F32))
                 for i in range(n))


def _sb_block(q, k, v, u2, carry, acc, vis):
    return _sb_blocks([q], [k], [v], u2, ((carry, acc),), vis)[0]


def _suffix_matrix2(c):
    row = lax.broadcasted_iota(I32, (2 * c, c), 0)
    col = lax.broadcasted_iota(I32, (2 * c, c), 1)
    return (jnp.where(row < c, row, row - c) > col).astype(BF16)


def _sb_prompt_kernel(q_ref, k_ref, v_ref, o_ref, *, bq, hg):
    qi = pl.program_id(2)
    u2 = _suffix_matrix2(bq)
    row = lax.broadcasted_iota(I32, (bq, bq), 0)
    col = lax.broadcasted_iota(I32, (bq, bq), 1)
    lanes = [slice(h * HEAD_DIM, (h + 1) * HEAD_DIM) for h in range(hg)]
    qs = [q_ref[:, ln] for ln in lanes]

    def blocks(s0, state, vis):
        return _sb_blocks(qs, [k_ref[pl.ds(s0, bq), ln] for ln in lanes], [v_ref[pl.ds(s0, bq), ln] for ln in lanes],
                          u2, state, vis)

    state = tuple((jnp.zeros((bq, 1), F32), jnp.zeros((bq, HEAD_DIM), F32)) for _ in range(hg))
    state = blocks(pl.multiple_of(qi * bq, bq), state, col < row)
    state = lax.fori_loop(0, qi, lambda i, st: blocks(pl.multiple_of((qi - 1 - i) * bq, bq), st, None), state)
    for h in range(hg):
        o_ref[:, lanes[h]] = state[h][1].astype(o_ref.dtype)


def sb_attention_prompt(q, k, v, batch, t):
    bq = _pick(t, (256, 128))
    nq = t // bq
    hg = HEADS_PER_STEP
    return pl.pallas_call(
        functools.partial(_sb_prompt_kernel, bq=bq, hg=hg),
        grid=(batch, N_HEADS // hg, nq),
        in_specs=[pl.BlockSpec((bq, hg * HEAD_DIM), lambda b, h, i: (b * nq + i, h)),
                  pl.BlockSpec((t, hg * HEAD_DIM), lambda b, h, i: (b, h)),
                  pl.BlockSpec((t, hg * HEAD_DIM), lambda b, h, i: (b, h))],
        out_specs=pl.BlockSpec((bq, hg * HEAD_DIM), lambda b, h, i: (b * nq + i, h)),
        out_shape=jax.ShapeDtypeStruct((batch * t, D_MODEL), BF16),
        compiler_params=_cparams(("parallel", "parallel", "arbitrary")),
        name="sb_attention_prompt",
    )(q, k, v)


def _block_diag_queries(q, db, t):
    q5 = q.reshape(db, t, N_HEADS, 1, HEAD_DIM)
    eye = jnp.eye(N_HEADS, dtype=jnp.bool_).reshape(1, 1, N_HEADS, N_HEADS, 1)
    return jnp.where(eye, q5, jnp.zeros_like(q5)).reshape(db, t * N_HEADS, D_MODEL)


def _extract_block_diag(acc, t):
    r = lax.broadcasted_iota(I32, acc.shape, 0)
    c = lax.broadcasted_iota(I32, acc.shape, 1)
    keep = (r % N_HEADS) == (c // HEAD_DIM)
    return jnp.sum(jnp.where(keep, acc, 0.0).reshape(t, N_HEADS, acc.shape[1]), axis=1)


N_HEAD_GROUPS = N_HEADS // HEAD_GROUP


def _load_page(group_refs):
    parts = []
    for ref in group_refs:
        rows = ref.reshape(PAGE_SIZE * HEAD_GROUP, HEAD_DIM)
        parts += [rows[pl.ds(h, PAGE_SIZE, stride=HEAD_GROUP), :] for h in range(HEAD_GROUP)]
    return jnp.concatenate(parts, axis=-1).astype(BF16)


def _page_refs(cache_refs, r):
    base = 2 * N_HEAD_GROUPS * r
    return cache_refs[base:base + N_HEAD_GROUPS], cache_refs[base + N_HEAD_GROUPS:base + 2 * N_HEAD_GROUPS]


def _cache_specs(pages, page_of):
    specs = []
    for r in range(pages):
        for _ in range(2):
            for g in range(N_HEAD_GROUPS):
                specs.append(pl.BlockSpec((None, PAGE_SIZE, HEAD_GROUP, HEAD_DIM),
                                          lambda b, j, pt, r=r, g=g: (page_of(b, j, pt, r), 0, g, 0)))
    return specs


def _cache_args(cache_k, cache_v, pages):
    return ([cache_k] * N_HEAD_GROUPS + [cache_v] * N_HEAD_GROUPS) * pages


def _sb_sample_kernel(pt_ref, q_ref, *refs, t, pages):
    n_cache = 2 * N_HEAD_GROUPS * pages
    cache_refs = refs[:n_cache]
    kn_ref, vn_ref, o_ref, carry_ref, acc_ref = refs[n_cache:]
    j = pl.program_id(1)
    rows = t * N_HEADS
    q = q_ref[...]

    @pl.when(j == 0)
    def _():
        r = lax.broadcasted_iota(I32, (rows, PAGE_SIZE), 0)
        c = lax.broadcasted_iota(I32, (rows, PAGE_SIZE), 1)
        carry, acc = _sb_block(q, kn_ref[...].astype(BF16), vn_ref[...].astype(BF16), _suffix_matrix2(PAGE_SIZE),
                               jnp.zeros((rows, 1), F32), jnp.zeros((rows, D_MODEL), F32), c < r // N_HEADS)
        carry_ref[...] = carry
        acc_ref[...] = acc

    @pl.when(j > 0)
    def _():
        order = range(pages - 1, -1, -1)
        k = jnp.concatenate([_load_page(_page_refs(cache_refs, r)[0]) for r in order], axis=0)
        v = jnp.concatenate([_load_page(_page_refs(cache_refs, r)[1]) for r in order], axis=0)
        carry, acc = _sb_block(q, k, v, _suffix_matrix2(pages * PAGE_SIZE), carry_ref[...], acc_ref[...], None)
        carry_ref[...] = carry
        acc_ref[...] = acc

    @pl.when(j == pl.num_programs(1) - 1)
    def _():
        o_ref[...] = _extract_block_diag(acc_ref[...], t)


def _pad_new_page(x, db, t):
    return jnp.pad(x.reshape(db, t, D_MODEL), ((0, 0), (0, PAGE_SIZE - t), (0, 0)))


def sb_attention_sample(q, k_new, v_new, cache_k, cache_v, page_table, db, t):
    n_pages = page_table.shape[1]
    pages = _pick(n_pages, (SAMPLE_PAGES_PER_STEP, 2, 1))
    rows = t * N_HEADS
    qbd = _block_diag_queries(q, db, t)

    def page_of(b, j, pt, r):
        return pt[b, n_pages - 1 - (jnp.maximum(j, 1) - 1) * pages - r]

    grid_spec = pltpu.PrefetchScalarGridSpec(
        num_scalar_prefetch=1,
        grid=(db, n_pages // pages + 1),
        in_specs=[pl.BlockSpec((None, rows, D_MODEL), lambda b, j, pt: (b, 0, 0))]
        + _cache_specs(pages, page_of)
        + [pl.BlockSpec((None, PAGE_SIZE, D_MODEL), lambda b, j, pt: (b, 0, 0)),
           pl.BlockSpec((None, PAGE_SIZE, D_MODEL), lambda b, j, pt: (b, 0, 0))],
        out_specs=pl.BlockSpec((None, t, D_MODEL), lambda b, j, pt: (b, 0, 0)),
        scratch_shapes=[pltpu.VMEM((rows, 1), F32), pltpu.VMEM((rows, D_MODEL), F32)],
    )
    out = pl.pallas_call(
        functools.partial(_sb_sample_kernel, t=t, pages=pages),
        grid_spec=grid_spec,
        out_shape=jax.ShapeDtypeStruct((db, t, D_MODEL), F32),
        compiler_params=_cparams(("parallel", "arbitrary")),
        name="sb_attention_sample",
    )(page_table, qbd, *_cache_args(cache_k, cache_v, pages), _pad_new_page(k_new, db, t), _pad_new_page(v_new, db, t))
    return out.reshape(db * t, D_MODEL)


def _pool_kernel(x_ref, halo_ref, g_ref, w_ref, ps_ref, o_ref, ext_ref, *, bm, blocks_per_seq, halo_is_state, pos0):
    i = pl.program_id(0)
    g = g_ref[...]

    def norm(x):
        ms = jnp.mean(x * x, axis=-1, keepdims=True)
        return x * lax.rsqrt(ms + RMS_EPS) * g

    x = x_ref[...]
    h = norm(x)
    if halo_is_state:
        halo = halo_ref[...]
    else:
        halo = jnp.where(i % blocks_per_seq == 0, 0.0, norm(halo_ref[...]))
    ext_ref[0:POOL_HALO, :] = halo
    ext_ref[POOL_HALO:POOL_HALO + bm, :] = h
    t_in_seq = (i % blocks_per_seq) * bm + lax.broadcasted_iota(I32, (bm, 1), 0) + pos0
    for gi, win in enumerate(POOL_WINDOWS):
        c0 = gi * POOL_GROUP_DIM
        c1 = c0 + POOL_GROUP_DIM
        wsum = h[:, c0:c1]
        for d in range(1, win):
            wsum = wsum + ext_ref[POOL_HALO - d:POOL_HALO - d + bm, c0:c1]
        count = jnp.minimum(t_in_seq + 1, win).astype(F32)
        pooled = wsum / count - h[:, c0:c1]
        y = jnp.dot(pooled.astype(BF16), w_ref[gi], preferred_element_type=F32)
        o_ref[:, c0:c1] = x[:, c0:c1] + y * ps_ref[:, c0:c1]


def pool_layer(x, halo_src, gain, w_pool, pool_scale, *, seq_len, halo_is_state, pos0):
    m, d = x.shape
    bm = _pick(seq_len, (256, 128, 64, 32, 16, 8))
    blocks_per_seq = seq_len // bm
    if halo_is_state:
        halo_spec = pl.BlockSpec((POOL_HALO, d), lambda i: (i, 0))
    else:
        per = bm // POOL_HALO
        halo_spec = pl.BlockSpec((POOL_HALO, d), lambda i: (jnp.maximum(i * per - 1, 0), 0))
    return pl.pallas_call(
        functools.partial(_pool_kernel, bm=bm, blocks_per_seq=blocks_per_seq, halo_is_state=halo_is_state, pos0=pos0),
        grid=(m // bm,),
        in_specs=[pl.BlockSpec((bm, d), lambda i: (i, 0)),
                  halo_spec,
                  pl.BlockSpec((1, d), lambda i: (0, 0)),
                  pl.BlockSpec(w_pool.shape, lambda i: (0, 0, 0)),
                  pl.BlockSpec((1, d), lambda i: (0, 0))],
        out_specs=pl.BlockSpec((bm, d), lambda i: (i, 0)),
        out_shape=jax.ShapeDtypeStruct((m, d), F32),
        scratch_shapes=[pltpu.VMEM((POOL_HALO + bm, d), F32)],
        compiler_params=_cparams(("parallel",)),
        name="pool_layer",
    )(x, halo_src, gain.reshape(1, d), w_pool, pool_scale.reshape(1, d))


def _sortable_key(x):
    bits = pltpu.bitcast(x, I32)
    return jnp.where(bits < 0, bits ^ jnp.int32(0x7FFFFFFF), bits)


def _lane_chunks(x):
    return [x[:, c * LANES:(c + 1) * LANES] for c in range(x.shape[1] // LANES)]


def _topk_bias(keys_ref, bias_ref, tie_ref, n_blocks, blk, q_pos, topk):
    rows = q_pos.shape[0]

    def count(pred):
        def body(kb, cnt):
            s0 = pl.multiple_of(kb * blk, blk)
            keys = keys_ref[:, pl.ds(s0, blk)]
            for c, kc in enumerate(_lane_chunks(keys)):
                pos = s0 + c * LANES + lax.broadcasted_iota(I32, (rows, LANES), 1)
                cnt = cnt + jnp.where(pred(kc, pos), 1.0, 0.0)
            return cnt
        cnt = lax.fori_loop(0, n_blocks, body, jnp.zeros((rows, LANES), F32))
        return jnp.sum(cnt, axis=-1, keepdims=True)

    def count_ge(cand):
        return count(lambda kc, pos: kc >= cand)

    thr = jnp.full((rows, 1), INT_MIN, I32)
    thr = jnp.where(count_ge(jnp.zeros((rows, 1), I32)) >= topk, 0, thr)

    def bit_body(i, thr):
        cand = thr | lax.shift_left(jnp.int32(1), 30 - i)
        return jnp.where(count_ge(cand) >= topk, cand, thr)

    thr = lax.fori_loop(0, 31, bit_body, thr)
    n_ge = count_ge(thr)
    n_gt = count(lambda kc, pos: kc > thr)
    need = topk - n_gt
    tie_ref[...] = jnp.full((rows, 1), 2 ** 30, I32)

    @pl.when(jnp.max(n_ge) > topk)
    def _():
        n_bits = max(1, int(math.ceil(math.log2(keys_ref.shape[1] + 1))))

        def cut_body(i, lo):
            cand = lo + lax.shift_left(jnp.int32(1), n_bits - 1 - i)
            f = count(lambda kc, pos: (kc == thr) & (pos <= cand))
            return jnp.where(f < need, cand, lo)

        lo = lax.fori_loop(0, n_bits, cut_body, jnp.full((rows, 1), -1, I32))
        tie_ref[...] = lo + 1

    cut = tie_ref[...]

    def write_body(kb, carry):
        s0 = pl.multiple_of(kb * blk, blk)
        keys = keys_ref[:, pl.ds(s0, blk)]
        pos = s0 + lax.broadcasted_iota(I32, (rows, blk), 1)
        sel = ((keys > thr) | ((keys == thr) & (pos <= cut))) & (pos <= q_pos)
        bias_ref[:, pl.ds(s0, blk)] = jnp.where(sel, 0.0, NEG_BIG)
        return carry

    lax.fori_loop(0, n_blocks, write_body, 0)


def _online_softmax_blocks(qs, ks, vs, bias, state):
    n = len(qs)
    ss = [lax.dot_general(qs[i], ks[i], (((1,), (1,)), ((), ())), preferred_element_type=F32) for i in range(n)]
    ms, ls, scales, ps = [], [], [], []
    for i in range(n):
        m, l, _ = state[i]
        s = ss[i] * (ATTN_SCALE * LOG2E) + bias
        m_new = jnp.maximum(m, jnp.max(s, axis=-1, keepdims=True))
        a = jnp.exp2(m - m_new)
        p = jnp.exp2(s - m_new)
        ms.append(m_new)
        ls.append(a * l + jnp.sum(p, axis=-1, keepdims=True))
        scales.append(a)
        ps.append(p.astype(BF16))
    return tuple((ms[i], ls[i], scales[i] * state[i][2] + jnp.dot(ps[i], vs[i], preferred_element_type=F32))
                 for i in range(n))


def _online_softmax_block(q, k, v, bias, m, l, acc):
    return _online_softmax_blocks([q], [k], [v], bias, ((m, l, acc),))[0]


def _dsa_prompt_kernel(q_ref, k_ref, v_ref, iq_ref, ik_ref, iw_ref, o_ref, keys_ref, bias_ref, tie_ref, iwb_ref,
                       *, bq, topk, hg):
    qi = pl.program_id(1)
    n_blocks = qi + 1
    q_pos = qi * bq + lax.broadcasted_iota(I32, (bq, 1), 0)

    @pl.when(pl.program_id(2) == 0)
    def _():
        iw = iw_ref[...]
        for hh in range(N_IDX_HEADS):
            iwb_ref[hh] = jnp.broadcast_to(iw[:, hh:hh + 1], (bq, LANES))

        def score_body(kb, carry):
            s0 = pl.multiple_of(kb * bq, bq)
            ik = ik_ref[pl.ds(s0, bq), :]
            sc = jnp.zeros((bq, bq), F32)
            for hh in range(N_IDX_HEADS):
                d = lax.dot_general(iq_ref[:, hh * IDX_DIM:(hh + 1) * IDX_DIM], ik, (((1,), (1,)), ((), ())),
                                    preferred_element_type=F32)
                sc = sc + jnp.maximum(d, 0.0) * jnp.concatenate([iwb_ref[hh]] * (bq // LANES), axis=-1)
            pos = s0 + lax.broadcasted_iota(I32, (bq, bq), 1)
            sc = jnp.where(pos <= q_pos, sc, -jnp.inf)
            keys_ref[:, pl.ds(s0, bq)] = _sortable_key(sc)
            return carry

        lax.fori_loop(0, n_blocks, score_body, 0)
        _topk_bias(keys_ref, bias_ref, tie_ref, n_blocks, bq, q_pos, topk)

    lanes = [slice(h * HEAD_DIM, (h + 1) * HEAD_DIM) for h in range(hg)]
    qs = [q_ref[:, ln] for ln in lanes]

    def body(kb, state):
        s0 = pl.multiple_of(kb * bq, bq)
        return _online_softmax_blocks(qs, [k_ref[pl.ds(s0, bq), ln] for ln in lanes],
                                      [v_ref[pl.ds(s0, bq), ln] for ln in lanes], bias_ref[:, pl.ds(s0, bq)], state)

    state = tuple((jnp.full((bq, 1), NEG_BIG, F32), jnp.zeros((bq, 1), F32), jnp.zeros((bq, HEAD_DIM), F32))
                  for _ in range(hg))
    state = lax.fori_loop(0, n_blocks, body, state)
    for h in range(hg):
        o_ref[:, lanes[h]] = (state[h][2] / state[h][1]).astype(o_ref.dtype)


def dsa_attention_prompt(q, k, v, iq, ik, iw, batch, t):
    bq = _pick(t, (256, 128))
    nq = t // bq
    hg = HEADS_PER_STEP
    topk = min(TOPK_MAX, t // 4)
    n_iq = N_IDX_HEADS * IDX_DIM
    return pl.pallas_call(
        functools.partial(_dsa_prompt_kernel, bq=bq, topk=topk, hg=hg),
        grid=(batch, nq, N_HEADS // hg),
        in_specs=[pl.BlockSpec((bq, hg * HEAD_DIM), lambda b, i, h: (b * nq + i, h)),
                  pl.BlockSpec((t, hg * HEAD_DIM), lambda b, i, h: (b, h)),
                  pl.BlockSpec((t, hg * HEAD_DIM), lambda b, i, h: (b, h)),
                  pl.BlockSpec((bq, n_iq), lambda b, i, h: (b * nq + i, 0)),
                  pl.BlockSpec((t, IDX_DIM), lambda b, i, h: (b, 0)),
                  pl.BlockSpec((bq, N_IDX_HEADS), lambda b, i, h: (b * nq + i, 0))],
        out_specs=pl.BlockSpec((bq, hg * HEAD_DIM), lambda b, i, h: (b * nq + i, h)),
        out_shape=jax.ShapeDtypeStruct((batch * t, D_MODEL), BF16),
        scratch_shapes=[pltpu.VMEM((bq, t), I32), pltpu.VMEM((bq, t), F32), pltpu.VMEM((bq, 1), I32),
                        pltpu.VMEM((N_IDX_HEADS, bq, LANES), F32)],
        compiler_params=_cparams(("parallel", "arbitrary", "arbitrary")),
        name="dsa_attention_prompt",
    )(q, k, v, iq, ik, iw)


def _idx_score_sample_kernel(pt_ref, iq_ref, iw_ref, *refs, t, pages):
    ik_refs, ikn_ref, o_ref = refs[:pages], refs[pages], refs[pages + 1]
    j = pl.program_id(1)

    def scores(ik):
        d = lax.dot_general(iq_ref[...], ik.astype(BF16), (((1,), (1,)), ((), ())), preferred_element_type=F32)
        wd = jnp.maximum(d, 0.0) * iw_ref[...]
        return jnp.sum(wd.reshape(t, N_IDX_HEADS, PAGE_SIZE), axis=1)

    @pl.when(j < pl.num_programs(1) - 1)
    def _():
        for r in range(pages):
            o_ref[:, r * PAGE_SIZE:(r + 1) * PAGE_SIZE] = scores(ik_refs[r][...])

    @pl.when(j == pl.num_programs(1) - 1)
    def _():
        o_ref[:, 0:PAGE_SIZE] = scores(ikn_ref[...])
        if pages > 1:
            o_ref[:, PAGE_SIZE:] = jnp.zeros((t, (pages - 1) * PAGE_SIZE), F32)


def idx_scores_sample(iq, iw, ik_new, cache_kidx, page_table, db, t):
    n_pages = page_table.shape[1]
    pages = _pick(n_pages, (IDX_PAGES_PER_STEP, 8, 4, 2, 1))
    n_steps = n_pages // pages
    rows = t * N_IDX_HEADS
    iq_rows = iq.reshape(db, rows, IDX_DIM)
    iw_rows = iw.reshape(db, rows, 1)
    ikn = jnp.pad(ik_new.reshape(db, t, IDX_DIM), ((0, 0), (0, PAGE_SIZE - t), (0, 0)))
    page_specs = [pl.BlockSpec((None, PAGE_SIZE, IDX_DIM),
                               lambda b, j, pt, r=r: (pt[b, jnp.minimum(j, n_steps - 1) * pages + r], 0, 0))
                  for r in range(pages)]
    grid_spec = pltpu.PrefetchScalarGridSpec(
        num_scalar_prefetch=1,
        grid=(db, n_steps + 1),
        in_specs=[pl.BlockSpec((None, rows, IDX_DIM), lambda b, j, pt: (b, 0, 0)),
                  pl.BlockSpec((None, rows, 1), lambda b, j, pt: (b, 0, 0))]
        + page_specs
        + [pl.BlockSpec((None, PAGE_SIZE, IDX_DIM), lambda b, j, pt: (b, 0, 0))],
        out_specs=pl.BlockSpec((None, t, pages * PAGE_SIZE), lambda b, j, pt: (b, 0, j)),
    )
    return pl.pallas_call(
        functools.partial(_idx_score_sample_kernel, t=t, pages=pages),
        grid_spec=grid_spec,
        out_shape=jax.ShapeDtypeStruct((db, t, (n_pages + pages) * PAGE_SIZE), F32),
        compiler_params=_cparams(("parallel", "arbitrary")),
        name="idx_scores_sample",
    )(page_table, iq_rows, iw_rows, *([cache_kidx] * pages), ikn)


def _topk_sample_kernel(s_ref, qpos_ref, bias_ref, keys_ref, tie_ref, *, topk, n_blocks):
    q_pos = qpos_ref[...]
    rows = q_pos.shape[0]

    def fill(kb, carry):
        s0 = pl.multiple_of(kb * LANES, LANES)
        pos = s0 + lax.broadcasted_iota(I32, (rows, LANES), 1)
        sc = jnp.where(pos <= q_pos, s_ref[:, pl.ds(s0, LANES)], -jnp.inf)
        keys_ref[:, pl.ds(s0, LANES)] = _sortable_key(sc)
        return carry

    lax.fori_loop(0, n_blocks, fill, 0)
    _topk_bias(keys_ref, bias_ref, tie_ref, n_blocks, LANES, q_pos, topk)


def topk_bias_sample(scores, q_pos, topk):
    r, l = scores.shape
    return pl.pallas_call(
        functools.partial(_topk_sample_kernel, topk=topk, n_blocks=l // LANES),
        out_shape=jax.ShapeDtypeStruct((r, l), F32),
        scratch_shapes=[pltpu.VMEM((r, l), I32), pltpu.VMEM((r, 1), I32)],
        compiler_params=pltpu.CompilerParams(vmem_limit_bytes=VMEM_LIMIT_BYTES),
        name="topk_bias_sample",
    )(scores, q_pos)


def _dsa_sample_kernel(pt_ref, q_ref, bias_ref, *refs, t, pages):
    n_cache = 2 * N_HEAD_GROUPS * pages
    cache_refs = refs[:n_cache]
    kn_ref, vn_ref, o_ref, m_ref, l_ref, acc_ref = refs[n_cache:]
    j = pl.program_id(1)
    rows = t * N_HEADS
    q = q_ref[...]

    @pl.when(j == 0)
    def _():
        m_ref[...] = jnp.full((rows, 1), NEG_BIG, F32)
        l_ref[...] = jnp.zeros((rows, 1), F32)
        acc_ref[...] = jnp.zeros((rows, D_MODEL), F32)

    def head_rows(bias):
        return jnp.broadcast_to(bias[:, None, :], (t, N_HEADS, bias.shape[1])).reshape(rows, bias.shape[1])

    @pl.when(j < pl.num_programs(1) - 1)
    def _():
        k = jnp.concatenate([_load_page(_page_refs(cache_refs, r)[0]) for r in range(pages)], axis=0)
        v = jnp.concatenate([_load_page(_page_refs(cache_refs, r)[1]) for r in range(pages)], axis=0)
        m_ref[...], l_ref[...], acc_ref[...] = _online_softmax_block(
            q, k, v, head_rows(bias_ref[...]), m_ref[...], l_ref[...], acc_ref[...])

    @pl.when(j == pl.num_programs(1) - 1)
    def _():
        m, l, acc = _online_softmax_block(q, kn_ref[...].astype(BF16), vn_ref[...].astype(BF16),
                                          head_rows(bias_ref[:, 0:PAGE_SIZE]), m_ref[...], l_ref[...], acc_ref[...])
        o_ref[...] = _extract_block_diag(acc / l, t)


def dsa_attention_sample(q, bias, k_new, v_new, cache_k, cache_v, page_table, db, t):
    n_pages = page_table.shape[1]
    pages = _pick(n_pages, (SAMPLE_PAGES_PER_STEP, 2, 1))
    n_steps = n_pages // pages
    assert bias.shape[2] >= (n_steps + 1) * pages * PAGE_SIZE
    rows = t * N_HEADS
    qbd = _block_diag_queries(q, db, t)

    def page_of(b, j, pt, r):
        return pt[b, jnp.minimum(j, n_steps - 1) * pages + r]

    grid_spec = pltpu.PrefetchScalarGridSpec(
        num_scalar_prefetch=1,
        grid=(db, n_steps + 1),
        in_specs=[pl.BlockSpec((None, rows, D_MODEL), lambda b, j, pt: (b, 0, 0)),
                  pl.BlockSpec((None, t, pages * PAGE_SIZE), lambda b, j, pt: (b, 0, j))]
        + _cache_specs(pages, page_of)
        + [pl.BlockSpec((None, PAGE_SIZE, D_MODEL), lambda b, j, pt: (b, 0, 0)),
           pl.BlockSpec((None, PAGE_SIZE, D_MODEL), lambda b, j, pt: (b, 0, 0))],
        out_specs=pl.BlockSpec((None, t, D_MODEL), lambda b, j, pt: (b, 0, 0)),
        scratch_shapes=[pltpu.VMEM((rows, 1), F32), pltpu.VMEM((rows, 1), F32), pltpu.VMEM((rows, D_MODEL), F32)],
    )
    out = pl.pallas_call(
        functools.partial(_dsa_sample_kernel, t=t, pages=pages),
        grid_spec=grid_spec,
        out_shape=jax.ShapeDtypeStruct((db, t, D_MODEL), F32),
        compiler_params=_cparams(("parallel", "arbitrary")),
        name="dsa_attention_sample",
    )(page_table, qbd, bias, *_cache_args(cache_k, cache_v, pages), _pad_new_page(k_new, db, t),
      _pad_new_page(v_new, db, t))
    return out.reshape(db * t, D_MODEL)


def _ffn(x, f_norm, wg, wu, wd):
    h = rmsnorm_rows(x, f_norm)
    return matmul_residual(ffn_up(h, wg, wu), wd, x)


def _sb_projections(x, attn_norm, w_qkv, q_norm, k_norm):
    h = rmsnorm_rows(x, attn_norm)
    (q,) = project(h, w_qkv, 0, D_MODEL, (BF16,), gain=q_norm)
    k32, k16 = project(h, w_qkv, D_MODEL, D_MODEL, (F32, BF16), gain=k_norm)
    v32, v16 = project(h, w_qkv, 2 * D_MODEL, D_MODEL, (F32, BF16))
    return q, k32, k16, v32, v16


def _dsa_projections(x, attn_norm, w_in, w_small, q_norm, k_norm, rope_qk, rope_idx):
    h = rmsnorm_rows(x, attn_norm)
    (q,) = project(h, w_in, 0, D_MODEL, (BF16,), gain=q_norm, rope=rope_qk, rope_shift=ROT_DIM // 2)
    k32, k16 = project(h, w_in, D_MODEL, D_MODEL, (F32, BF16), gain=k_norm, rope=rope_qk, rope_shift=ROT_DIM // 2)
    v32, v16 = project(h, w_in, 2 * D_MODEL, D_MODEL, (F32, BF16))
    (iq,) = project(h, w_in, 3 * D_MODEL, N_IDX_HEADS * IDX_DIM, (BF16,), rope=rope_idx, rope_shift=IDX_ROT_DIM // 2)
    (tail_rot,) = project(h, w_small, 0, LANES, (F32,), rope=rope_idx, rope_shift=IDX_ROT_DIM // 2)
    (tail_raw,) = project(h, w_small, 0, LANES, (F32,))
    ik32 = tail_rot[:, :IDX_DIM]
    iw = tail_raw[:, IDX_DIM:IDX_DIM + N_IDX_HEADS]
    return q, k32, k16, v32, v16, iq, ik32, iw


def kernel(x_prompt, x_sample, cache_k_0, cache_v_0, state_pool_1, cache_k_2, cache_v_2, cache_kidx_2, cache_k_3, cache_v_3, page_table, attn_norm_0, w_qkv_0, q_norm_0, k_norm_0, w_o_0, ffn_norm_0, w_gate_0, w_up_0, w_down_0, attn_norm_1, w_pool_1, pool_scale_1, ffn_norm_1, w_gate_1, w_up_1, w_down_1, attn_norm_2, w_in_2, q_norm_2, k_norm_2, w_o_2, ffn_norm_2, w_gate_2, w_up_2, w_down_2, attn_norm_3, w_qkv_3, q_norm_3, k_norm_3, w_o_3, ffn_norm_3, w_gate_3, w_up_3, w_down_3):
    batch, t, d = x_prompt.shape
    db, dt, _ = x_sample.shape
    n_phys = cache_k_0.shape[0]
    n_pages = page_table.shape[1]
    past = n_pages * PAGE_SIZE
    mp, ms = batch * t, db * dt
    xp = x_prompt.reshape(mp, d)
    xs = x_sample.reshape(ms, d)

    def bf(w):
        return w.astype(BF16)

    def heads(a, b_, t_):
        return a.reshape(b_, t_, N_HEADS, HEAD_DIM)

    def sb_layer(xp, xs, cache_k, cache_v, attn_norm, w_qkv, q_norm, k_norm, w_o):
        w_qkv, w_o = bf(w_qkv), bf(w_o)
        q, k32, k16, v32, v16 = _sb_projections(xp, attn_norm, w_qkv, q_norm, k_norm)
        xp = matmul_residual(sb_attention_prompt(q, k16, v16, batch, t), w_o, xp)
        qs, ks32, _, vs32, _ = _sb_projections(xs, attn_norm, w_qkv, q_norm, k_norm)
        o_s = sb_attention_sample(qs, ks32, vs32, cache_k, cache_v, page_table, db, dt)
        xs = matmul_residual(o_s.astype(BF16), w_o, xs)
        return xp, xs, (heads(k32, batch, t), heads(v32, batch, t)), (heads(ks32, db, dt), heads(vs32, db, dt))

    xp, xs, (k0_p, v0_p), (k0_s, v0_s) = sb_layer(xp, xs, cache_k_0, cache_v_0, attn_norm_0, w_qkv_0, q_norm_0,
                                                   k_norm_0, w_o_0)
    xp = _ffn(xp, ffn_norm_0, bf(w_gate_0), bf(w_up_0), bf(w_down_0))
    xs = _ffn(xs, ffn_norm_0, bf(w_gate_0), bf(w_up_0), bf(w_down_0))

    w_pool = bf(w_pool_1)
    pool1_p = rmsnorm_rows(xp.reshape(batch, t, d)[:, t - POOL_HALO:].reshape(batch * POOL_HALO, d), attn_norm_1,
                           F32).reshape(batch, POOL_HALO, d)[:, POOL_HALO - POOL_STATE:]
    hs_new = rmsnorm_rows(xs, attn_norm_1, F32)
    pool1_s = jnp.concatenate([state_pool_1, hs_new.reshape(db, dt, d)], axis=1)[:, -POOL_STATE:]
    xp = pool_layer(xp, xp, attn_norm_1, w_pool, pool_scale_1, seq_len=t, halo_is_state=False, pos0=0)
    rows_s = 8
    xs_pad = jnp.pad(xs.reshape(db, dt, d), ((0, 0), (0, rows_s - dt), (0, 0))).reshape(db * rows_s, d)
    state_pad = jnp.pad(state_pool_1, ((0, 0), (POOL_HALO - POOL_STATE, 0), (0, 0))).reshape(db * POOL_HALO, d)
    xs = pool_layer(xs_pad, state_pad, attn_norm_1, w_pool, pool_scale_1, seq_len=rows_s, halo_is_state=True,
                    pos0=PAST_LEN).reshape(db, rows_s, d)[:, :dt].reshape(ms, d)
    xp = _ffn(xp, ffn_norm_1, bf(w_gate_1), bf(w_up_1), bf(w_down_1))
    xs = _ffn(xs, ffn_norm_1, bf(w_gate_1), bf(w_up_1), bf(w_down_1))

    w_in, w_o2 = bf(w_in_2), bf(w_o_2)
    n_main = 3 * D_MODEL + N_IDX_HEADS * IDX_DIM
    w_small = jnp.pad(w_in[:, n_main:], ((0, 0), (0, LANES - (w_in.shape[1] - n_main))))
    pos_p = jnp.arange(t, dtype=I32)
    pos_s = jnp.tile(past + jnp.arange(dt, dtype=I32), db)
    q, k32, k16, v32, v16, iq, ik32, iw = _dsa_projections(
        xp, attn_norm_2, w_in, w_small, q_norm_2, k_norm_2,
        rope_tables(pos_p, ROT_DIM, HEAD_DIM), rope_tables(pos_p, IDX_ROT_DIM, IDX_DIM))
    iw_scaled = iw * (IDX_W_SCALE * IDX_SCALE)
    o_p = dsa_attention_prompt(q, k16, v16, iq, ik32.astype(BF16), iw_scaled, batch, t)
    k2_p, v2_p, kidx2_p = heads(k32, batch, t), heads(v32, batch, t), ik32.reshape(batch, t, IDX_DIM)
    xp = matmul_residual(o_p, w_o2, xp)

    qs, ks32, _, vs32, _, iqs, iks32, iws = _dsa_projections(
        xs, attn_norm_2, w_in, w_small, q_norm_2, k_norm_2,
        rope_tables(pos_s, ROT_DIM, HEAD_DIM), rope_tables(pos_s, IDX_ROT_DIM, IDX_DIM))
    scores = idx_scores_sample(iqs, iws * (IDX_W_SCALE * IDX_SCALE), iks32, cache_kidx_2, page_table, db, dt)
    topk_s = min(TOPK_MAX, (past + dt) // 4)
    bias_s = topk_bias_sample(scores.reshape(ms, -1), pos_s.reshape(ms, 1), topk_s).reshape(db, dt, -1)
    o_s = dsa_attention_sample(qs, bias_s, ks32, vs32, cache_k_2, cache_v_2, page_table, db, dt)
    k2_s, v2_s, kidx2_s = heads(ks32, db, dt), heads(vs32, db, dt), iks32.reshape(db, dt, IDX_DIM)
    xs = matmul_residual(o_s.astype(BF16), w_o2, xs)
    xp = _ffn(xp, ffn_norm_2, bf(w_gate_2), bf(w_up_2), bf(w_down_2))
    xs = _ffn(xs, ffn_norm_2, bf(w_gate_2), bf(w_up_2), bf(w_down_2))

    xp, xs, (k3_p, v3_p), (k3_s, v3_s) = sb_layer(xp, xs, cache_k_3, cache_v_3, attn_norm_3, w_qkv_3, q_norm_3,
                                                   k_norm_3, w_o_3)
    xp = _ffn(xp, ffn_norm_3, bf(w_gate_3), bf(w_up_3), bf(w_down_3))
    xs = _ffn(xs, ffn_norm_3, bf(w_gate_3), bf(w_up_3), bf(w_down_3))

    return (xp.reshape(batch, t, d), xs.reshape(db, dt, d), k0_p, v0_p, k0_s, v0_s, pool1_p, pool1_s,
            k2_p, v2_p, kidx2_p, k2_s, v2_s, kidx2_s, k3_p, v3_p, k3_s, v3_s)
```

```python
import functools
import math

import jax
import jax.numpy as jnp
from jax import lax
from jax.experimental import pallas as pl
from jax.experimental.pallas import tpu as pltpu

F32 = jnp.float32
BF16 = jnp.bfloat16
I32 = jnp.int32

D_MODEL = 2048
N_HEADS = 16
HEAD_DIM = 128
ROT_DIM = 32
ROPE_THETA = 500000.0
POOL_WINDOWS = (2, 4, 8, 16)
POOL_GROUP_DIM = 512
POOL_STATE = 15
N_IDX_HEADS = 16
IDX_DIM = 64
IDX_ROT_DIM = 16
TOPK_MAX = 256
PAGE_SIZE = 128
PAST_LEN = 16384
RMS_EPS = 1e-6
ATTN_SCALE = HEAD_DIM ** -0.5
IDX_SCALE = IDX_DIM ** -0.5
IDX_W_SCALE = N_IDX_HEADS ** -0.5
LOG2E = 1.4426950408889634

HEADS_PER_STEP = 4
SAMPLE_PAGES_PER_STEP = 4
IDX_PAGES_PER_STEP = 16
HEAD_GROUP = 8
PROJ_ROW_CHUNK = 256

LANES = 128
POOL_HALO = 16
VMEM_LIMIT_BYTES = 56 * 1024 * 1024
NEG_BIG = -1e30
INT_MIN = -2 ** 31
SB_DEAD_LOG2 = -160.0


def _cparams(sem):
    return pltpu.CompilerParams(dimension_semantics=sem, vmem_limit_bytes=VMEM_LIMIT_BYTES)


def _pick(n, cands):
    for c in cands:
        if n % c == 0:
            return c
    return n


def _rmsnorm_kernel(x_ref, g_ref, o_ref):
    x = x_ref[...]
    ms = jnp.mean(x * x, axis=-1, keepdims=True)
    o_ref[...] = (x * lax.rsqrt(ms + RMS_EPS) * g_ref[...]).astype(o_ref.dtype)


def rmsnorm_rows(x, g, out_dtype=BF16):
    m, d = x.shape
    bm = _pick(m, (512, 256, 128, 64, 32, 16, 8))
    return pl.pallas_call(
        _rmsnorm_kernel,
        grid=(m // bm,),
        in_specs=[pl.BlockSpec((bm, d), lambda i: (i, 0)),
                  pl.BlockSpec((1, d), lambda i: (0, 0))],
        out_specs=pl.BlockSpec((bm, d), lambda i: (i, 0)),
        out_shape=jax.ShapeDtypeStruct((m, d), out_dtype),
        compiler_params=_cparams(("parallel",)),
        name="rmsnorm",
    )(x, g.reshape(1, d))


def _proj_kernel(*refs, norm, rope_shift, scale, n_out, row_chunk):
    x_ref, w_ref = refs[0], refs[1]
    pos = 2
    g_ref = None
    if norm:
        g_ref = refs[pos]
        pos += 1
    tabs = None
    if rope_shift:
        tabs = refs[pos:pos + 3]
        pos += 3
    out_refs = refs[pos:pos + n_out]
    bm, bn = out_refs[0].shape
    w = w_ref[...]
    chunks = [slice(r, r + row_chunk) for r in range(0, bm, row_chunk)]
    ys = [jnp.dot(x_ref[rows, :], w, preferred_element_type=F32) for rows in chunks]
    for rows, y in zip(chunks, ys):
        parts = []
        for h in range(bn // LANES):
            yh = y[:, h * LANES:(h + 1) * LANES]
            if norm:
                ms = jnp.mean(yh * yh, axis=-1, keepdims=True)
                yh = yh * lax.rsqrt(ms + RMS_EPS) * g_ref[:, h * LANES:(h + 1) * LANES]
            if rope_shift:
                c, sa, sb = (tab[rows, :] for tab in tabs)
                yh = yh * c + pltpu.roll(yh, rope_shift, 1) * sa + pltpu.roll(yh, LANES - rope_shift, 1) * sb
            if scale is not None:
                yh = yh * scale
            parts.append(yh)
        y = jnp.concatenate(parts, axis=-1)
        for o_ref in out_refs:
            o_ref[rows, :] = y.astype(o_ref.dtype)


def project(h, w, col0, n, out_dtypes, *, gain=None, rope=None, rope_shift=0, scale=None, bn=512):
    m, k = h.shape
    bm = _pick(m, (1024, 512, 256, 128, 64, 32, 16))
    bn = min(bn, n)
    assert n % bn == 0 and col0 % bn == 0 and bn % LANES == 0
    off = col0 // bn
    in_specs = [pl.BlockSpec((bm, k), lambda i, j: (i, 0)),
                pl.BlockSpec((k, bn), lambda i, j: (0, j + off))]
    args = [h, w]
    if gain is not None:
        in_specs.append(pl.BlockSpec((1, bn), lambda i, j: (0, 0)))
        args.append(jnp.tile(gain.reshape(1, -1), (1, bn // gain.shape[-1])))
    if rope is not None:
        period = rope[0].shape[0] // bm
        for tab in rope:
            in_specs.append(pl.BlockSpec((bm, LANES), lambda i, j: (i % period, 0)))
            args.append(tab)
    outs = pl.pallas_call(
        functools.partial(_proj_kernel, norm=gain is not None, rope_shift=rope_shift if rope is not None else 0,
                          scale=scale, n_out=len(out_dtypes), row_chunk=_pick(bm, (PROJ_ROW_CHUNK,))),
        grid=(m // bm, n // bn),
        in_specs=in_specs,
        out_specs=[pl.BlockSpec((bm, bn), lambda i, j: (i, j)) for _ in out_dtypes],
        out_shape=[jax.ShapeDtypeStruct((m, n), dt) for dt in out_dtypes],
        compiler_params=_cparams(("parallel", "parallel")),
        name="project",
    )(*args)
    return outs


def rope_tables(pos, rot_dim, head_dim):
    half = rot_dim // 2
    inv_freq = ROPE_THETA ** (-jnp.arange(half, dtype=F32) / half)
    ang = pos.astype(F32)[:, None] * inv_freq[None, :]
    cos, sin = jnp.cos(ang), jnp.sin(ang)
    p = pos.shape[0]
    ones = jnp.ones((p, head_dim - rot_dim), F32)
    zeros_h = jnp.zeros((p, half), F32)
    zeros_r = jnp.zeros((p, head_dim - rot_dim), F32)
    c = jnp.concatenate([cos, cos, ones], axis=-1)
    sa = jnp.concatenate([zeros_h, sin, zeros_r], axis=-1)
    sb = jnp.concatenate([-sin, zeros_h, zeros_r], axis=-1)
    reps = LANES // head_dim
    return tuple(jnp.tile(t, (1, reps)) for t in (c, sa, sb))


def _mm_res_kernel(a_ref, w_ref, r_ref, o_ref, w_bf):
    @pl.when(pl.program_id(1) == 0)
    def _():
        w_bf[...] = w_ref[...].astype(BF16)

    o_ref[...] = r_ref[...] + jnp.dot(a_ref[...], w_bf[...], preferred_element_type=F32)


def matmul_residual(a, w, res):
    m, k = a.shape
    n = w.shape[1]
    bm = _pick(m, (512, 256, 128, 64, 32, 16))
    bn = _pick(n, (512, 256, 128))
    return pl.pallas_call(
        _mm_res_kernel,
        grid=(n // bn, m // bm),
        in_specs=[pl.BlockSpec((bm, k), lambda j, i: (i, 0)),
                  pl.BlockSpec((k, bn), lambda j, i: (0, j)),
                  pl.BlockSpec((bm, bn), lambda j, i: (i, j))],
        out_specs=pl.BlockSpec((bm, bn), lambda j, i: (i, j)),
        out_shape=jax.ShapeDtypeStruct((m, n), F32),
        scratch_shapes=[pltpu.VMEM((k, bn), BF16)],
        compiler_params=_cparams(("arbitrary", "arbitrary")),
        name="matmul_residual",
    )(a, w, res)


def _ffn_up_kernel(x_ref, wg_ref, wu_ref, o_ref, wg_bf, wu_bf):
    @pl.when(pl.program_id(1) == 0)
    def _():
        wg_bf[...] = wg_ref[...].astype(BF16)
        wu_bf[...] = wu_ref[...].astype(BF16)

    x = x_ref[...]
    g = jnp.dot(x, wg_bf[...], preferred_element_type=F32)
    u = jnp.dot(x, wu_bf[...], preferred_element_type=F32)
    o_ref[...] = (g * (1.0 / (1.0 + jnp.exp(-g))) * u).astype(o_ref.dtype)


def ffn_up(h, wg, wu):
    m, k = h.shape
    n = wg.shape[1]
    bm = _pick(m, (1024, 512, 256, 128, 64, 32, 16))
    bn = _pick(n, (512, 256, 128))
    return pl.pallas_call(
        _ffn_up_kernel,
        grid=(n // bn, m // bm),
        in_specs=[pl.BlockSpec((bm, k), lambda j, i: (i, 0)),
                  pl.BlockSpec((k, bn), lambda j, i: (0, j)),
                  pl.BlockSpec((k, bn), lambda j, i: (0, j))],
        out_specs=pl.BlockSpec((bm, bn), lambda j, i: (i, j)),
        out_shape=jax.ShapeDtypeStruct((m, n), BF16),
        scratch_shapes=[pltpu.VMEM((k, bn), BF16), pltpu.VMEM((k, bn), BF16)],
        compiler_params=_cparams(("arbitrary", "arbitrary")),
        name="ffn_up",
    )(h, wg, wu)


def _sb_blocks(qs, ks, vs, u2, state, vis):
    n = len(qs)
    zs = [lax.dot_general(qs[i], ks[i], (((1,), (1,)), ((), ())), preferred_element_type=F32) for i in range(n)]
    log_betas, sums, hls = [], [], []
    for z in zs:
        z = z * (ATTN_SCALE * LOG2E)
        neg_abs = pltpu.bitcast(pltpu.bitcast(z, I32) | jnp.int32(INT_MIN), F32)
        log_beta = jnp.minimum(z, 0.0) - jnp.log(1.0 + jnp.exp2(neg_abs)) * LOG2E
        log_keep = log_beta - z
        if vis is not None:
            log_keep = jnp.where(vis, log_keep, 0.0)
        hi = log_keep.astype(BF16)
        lo = (log_keep - hi.astype(F32)).astype(BF16)
        log_betas.append(log_beta)
        sums.append(jnp.sum(log_keep, axis=-1, keepdims=True))
        hls.append(jnp.concatenate([hi, lo], axis=-1))
    tails = [jnp.dot(hl, u2, preferred_element_type=F32) for hl in hls]
    ws = []
    for i in range(n):
        w = jnp.exp2(log_betas[i] + (tails[i] + state[i][0]))
        if vis is not None:
            w = jnp.where(vis, w, 0.0)
        ws.append(w.astype(BF16))
    return tuple((state[i][0] + sums[i], state[i][1] + jnp.dot(ws[i], vs[i], preferred_element_type=F32))
                 for i in range(n))


def _sb_block(q, k, v, u2, carry, acc, vis):
    return _sb_blocks([q], [k], [v], u2, ((carry, acc),), vis)[0]


def _sb_any_live(state):
    live = state[0][0]
    for carry, _ in state[1:]:
        live = jnp.maximum(live, carry)
    return (jnp.max(live) >= SB_DEAD_LOG2).astype(I32)


def _suffix_matrix2(c):
    row = lax.broadcasted_iota(I32, (2 * c, c), 0)
    col = lax.broadcasted_iota(I32, (2 * c, c), 1)
    return (jnp.where(row < c, row, row - c) > col).astype(BF16)


def _sb_prompt_kernel(q_ref, k_ref, v_ref, o_ref, *, bq, hg):
    qi = pl.program_id(2)
    u2 = _suffix_matrix2(bq)
    row = lax.broadcasted_iota(I32, (bq, bq), 0)
    col = lax.broadcasted_iota(I32, (bq, bq), 1)
    lanes = [slice(h * HEAD_DIM, (h + 1) * HEAD_DIM) for h in range(hg)]
    qs = [q_ref[:, ln] for ln in lanes]

    def blocks(s0, state, vis):
        return _sb_blocks(qs, [k_ref[pl.ds(s0, bq), ln] for ln in lanes], [v_ref[pl.ds(s0, bq), ln] for ln in lanes],
                          u2, state, vis)

    state = tuple((jnp.zeros((bq, 1), F32), jnp.zeros((bq, HEAD_DIM), F32)) for _ in range(hg))
    state = blocks(pl.multiple_of(qi * bq, bq), state, col < row)

    def step(c):
        i, _, st = c
        st = blocks(pl.multiple_of((qi - 1 - i) * bq, bq), st, None)
        return i + 1, _sb_any_live(st), st

    _, _, state = lax.while_loop(lambda c: (c[0] < qi) & (c[1] > 0), step, (jnp.int32(0), _sb_any_live(state), state))
    for h in range(hg):
        o_ref[:, lanes[h]] = state[h][1].astype(o_ref.dtype)


def sb_attention_prompt(q, k, v, batch, t):
    bq = _pick(t, (256, 128))
    nq = t // bq
    hg = HEADS_PER_STEP
    return pl.pallas_call(
        functools.partial(_sb_prompt_kernel, bq=bq, hg=hg),
        grid=(batch, N_HEADS // hg, nq),
        in_specs=[pl.BlockSpec((bq, hg * HEAD_DIM), lambda b, h, i: (b * nq + i, h)),
                  pl.BlockSpec((t, hg * HEAD_DIM), lambda b, h, i: (b, h)),
                  pl.BlockSpec((t, hg * HEAD_DIM), lambda b, h, i: (b, h))],
        out_specs=pl.BlockSpec((bq, hg * HEAD_DIM), lambda b, h, i: (b * nq + i, h)),
        out_shape=jax.ShapeDtypeStruct((batch * t, D_MODEL), BF16),
        compiler_params=_cparams(("parallel", "parallel", "arbitrary")),
        name="sb_attention_prompt",
    )(q, k, v)


def _block_diag_queries(q, db, t):
    q5 = q.reshape(db, t, N_HEADS, 1, HEAD_DIM)
    eye = jnp.eye(N_HEADS, dtype=jnp.bool_).reshape(1, 1, N_HEADS, N_HEADS, 1)
    return jnp.where(eye, q5, jnp.zeros_like(q5)).reshape(db, t * N_HEADS, D_MODEL)


def _extract_block_diag(acc, t):
    r = lax.broadcasted_iota(I32, acc.shape, 0)
    c = lax.broadcasted_iota(I32, acc.shape, 1)
    keep = (r % N_HEADS) == (c // HEAD_DIM)
    return jnp.sum(jnp.where(keep, acc, 0.0).reshape(t, N_HEADS, acc.shape[1]), axis=1)


N_HEAD_GROUPS = N_HEADS // HEAD_GROUP


def _load_page(group_refs):
    parts = []
    for ref in group_refs:
        rows = ref.reshape(PAGE_SIZE * HEAD_GROUP, HEAD_DIM)
        parts += [rows[pl.ds(h, PAGE_SIZE, stride=HEAD_GROUP), :] for h in range(HEAD_GROUP)]
    return jnp.concatenate(parts, axis=-1).astype(BF16)


def _page_refs(cache_refs, r):
    base = 2 * N_HEAD_GROUPS * r
    return cache_refs[base:base + N_HEAD_GROUPS], cache_refs[base + N_HEAD_GROUPS:base + 2 * N_HEAD_GROUPS]


def _cache_specs(pages, page_of):
    specs = []
    for r in range(pages):
        for _ in range(2):
            for g in range(N_HEAD_GROUPS):
                specs.append(pl.BlockSpec((None, PAGE_SIZE, HEAD_GROUP, HEAD_DIM),
                                          lambda b, j, pt, r=r, g=g: (page_of(b, j, pt, r), 0, g, 0)))
    return specs


def _cache_args(cache_k, cache_v, pages):
    return ([cache_k] * N_HEAD_GROUPS + [cache_v] * N_HEAD_GROUPS) * pages


def _load_flat_page(page_ref):
    return jnp.concatenate([page_ref[pl.ds(h, PAGE_SIZE, stride=N_HEADS), :] for h in range(N_HEADS)],
                           axis=-1).astype(BF16)


def _sb_sample_kernel(pt_ref, q_ref, kn_ref, vn_ref, ck_hbm, cv_hbm, o_ref, kbuf, vbuf, sem, carry_ref, acc_ref,
                      *, t, n_pages):
    b = pl.program_id(0)
    rows = t * N_HEADS
    q = q_ref[...]

    def page_copies(i, slot):
        page = pt_ref[b, n_pages - 1 - i]
        return (pltpu.make_async_copy(ck_hbm.at[page], kbuf.at[slot], sem.at[0, slot]),
                pltpu.make_async_copy(cv_hbm.at[page], vbuf.at[slot], sem.at[1, slot]))

    def start(i, slot):
        for cp in page_copies(i, slot):
            cp.start()

    def wait(i, slot):
        for cp in page_copies(i, slot):
            cp.wait()

    start(0, 0)
    r = lax.broadcasted_iota(I32, (rows, PAGE_SIZE), 0)
    c = lax.broadcasted_iota(I32, (rows, PAGE_SIZE), 1)
    carry, acc = _sb_block(q, kn_ref[...].astype(BF16), vn_ref[...].astype(BF16), _suffix_matrix2(PAGE_SIZE),
                           jnp.zeros((rows, 1), F32), jnp.zeros((rows, D_MODEL), F32), c < r // N_HEADS)
    carry_ref[...] = carry
    acc_ref[...] = acc
    u2 = _suffix_matrix2(PAGE_SIZE)

    def body(c):
        i, _ = c
        slot = lax.rem(i, 2)
        wait(i, slot)

        @pl.when(i + 1 < n_pages)
        def _():
            start(i + 1, 1 - slot)

        carry, acc = _sb_block(q, _load_flat_page(kbuf.at[slot]), _load_flat_page(vbuf.at[slot]), u2,
                               carry_ref[...], acc_ref[...], None)
        carry_ref[...] = carry
        acc_ref[...] = acc
        return i + 1, _sb_any_live(((carry, acc),))

    i_end, _ = lax.while_loop(lambda c: (c[0] < n_pages) & (c[1] > 0), body,
                              (jnp.int32(0), _sb_any_live(((carry, acc),))))

    @pl.when(i_end < n_pages)
    def _():
        wait(i_end, lax.rem(i_end, 2))

    o_ref[...] = _extract_block_diag(acc_ref[...], t)


def _pad_new_page(x, db, t):
    return jnp.pad(x.reshape(db, t, D_MODEL), ((0, 0), (0, PAGE_SIZE - t), (0, 0)))


def sb_attention_sample(q, k_new, v_new, cache_k, cache_v, page_table, db, t):
    n_phys = cache_k.shape[0]
    n_pages = page_table.shape[1]
    rows = t * N_HEADS
    qbd = _block_diag_queries(q, db, t)
    page_rows = PAGE_SIZE * N_HEADS

    def flat(cache):
        return cache.reshape(n_phys, page_rows, HEAD_DIM)

    grid_spec = pltpu.PrefetchScalarGridSpec(
        num_scalar_prefetch=1,
        grid=(db,),
        in_specs=[pl.BlockSpec((None, rows, D_MODEL), lambda b, pt: (b, 0, 0)),
                  pl.BlockSpec((None, PAGE_SIZE, D_MODEL), lambda b, pt: (b, 0, 0)),
                  pl.BlockSpec((None, PAGE_SIZE, D_MODEL), lambda b, pt: (b, 0, 0)),
                  pl.BlockSpec(memory_space=pl.ANY),
                  pl.BlockSpec(memory_space=pl.ANY)],
        out_specs=pl.BlockSpec((None, t, D_MODEL), lambda b, pt: (b, 0, 0)),
        scratch_shapes=[pltpu.VMEM((2, page_rows, HEAD_DIM), F32), pltpu.VMEM((2, page_rows, HEAD_DIM), F32),
                        pltpu.SemaphoreType.DMA((2, 2)),
                        pltpu.VMEM((rows, 1), F32), pltpu.VMEM((rows, D_MODEL), F32)],
    )
    out = pl.pallas_call(
        functools.partial(_sb_sample_kernel, t=t, n_pages=n_pages),
        grid_spec=grid_spec,
        out_shape=jax.ShapeDtypeStruct((db, t, D_MODEL), F32),
        compiler_params=_cparams(("arbitrary",)),
        name="sb_attention_sample",
    )(page_table, qbd, _pad_new_page(k_new, db, t), _pad_new_page(v_new, db, t), flat(cache_k), flat(cache_v))
    return out.reshape(db * t, D_MODEL)


def _pool_kernel(x_ref, halo_ref, g_ref, w_ref, ps_ref, o_ref, ext_ref, *, bm, blocks_per_seq, halo_is_state, pos0):
    i = pl.program_id(0)
    g = g_ref[...]

    def norm(x):
        ms = jnp.mean(x * x, axis=-1, keepdims=True)
        return x * lax.rsqrt(ms + RMS_EPS) * g

    x = x_ref[...]
    h = norm(x)
    if halo_is_state:
        halo = halo_ref[...]
    else:
        halo = jnp.where(i % blocks_per_seq == 0, 0.0, norm(halo_ref[...]))
    ext_ref[0:POOL_HALO, :] = halo
    ext_ref[POOL_HALO:POOL_HALO + bm, :] = h
    t_in_seq = (i % blocks_per_seq) * bm + lax.broadcasted_iota(I32, (bm, 1), 0) + pos0
    for gi, win in enumerate(POOL_WINDOWS):
        c0 = gi * POOL_GROUP_DIM
        c1 = c0 + POOL_GROUP_DIM
        wsum = h[:, c0:c1]
        for d in range(1, win):
            wsum = wsum + ext_ref[POOL_HALO - d:POOL_HALO - d + bm, c0:c1]
        count = jnp.minimum(t_in_seq + 1, win).astype(F32)
        pooled = wsum / count - h[:, c0:c1]
        y = jnp.dot(pooled.astype(BF16), w_ref[gi], preferred_element_type=F32)
        o_ref[:, c0:c1] = x[:, c0:c1] + y * ps_ref[:, c0:c1]


def pool_layer(x, halo_src, gain, w_pool, pool_scale, *, seq_len, halo_is_state, pos0):
    m, d = x.shape
    bm = _pick(seq_len, (256, 128, 64, 32, 16, 8))
    blocks_per_seq = seq_len // bm
    if halo_is_state:
        halo_spec = pl.BlockSpec((POOL_HALO, d), lambda i: (i, 0))
    else:
        per = bm // POOL_HALO
        halo_spec = pl.BlockSpec((POOL_HALO, d), lambda i: (jnp.maximum(i * per - 1, 0), 0))
    return pl.pallas_call(
        functools.partial(_pool_kernel, bm=bm, blocks_per_seq=blocks_per_seq, halo_is_state=halo_is_state, pos0=pos0),
        grid=(m // bm,),
        in_specs=[pl.BlockSpec((bm, d), lambda i: (i, 0)),
                  halo_spec,
                  pl.BlockSpec((1, d), lambda i: (0, 0)),
                  pl.BlockSpec(w_pool.shape, lambda i: (0, 0, 0)),
                  pl.BlockSpec((1, d), lambda i: (0, 0))],
        out_specs=pl.BlockSpec((bm, d), lambda i: (i, 0)),
        out_shape=jax.ShapeDtypeStruct((m, d), F32),
        scratch_shapes=[pltpu.VMEM((POOL_HALO + bm, d), F32)],
        compiler_params=_cparams(("parallel",)),
        name="pool_layer",
    )(x, halo_src, gain.reshape(1, d), w_pool, pool_scale.reshape(1, d))


def _sortable_key(x):
    bits = pltpu.bitcast(x, I32)
    return jnp.where(bits < 0, bits ^ jnp.int32(0x7FFFFFFF), bits)


def _lane_chunks(x):
    return [x[:, c * LANES:(c + 1) * LANES] for c in range(x.shape[1] // LANES)]


def _topk_bias(keys_ref, bias_ref, tie_ref, n_blocks, blk, q_pos, topk):
    rows = q_pos.shape[0]

    def count(pred):
        def body(kb, cnt):
            s0 = pl.multiple_of(kb * blk, blk)
            keys = keys_ref[:, pl.ds(s0, blk)]
            for c, kc in enumerate(_lane_chunks(keys)):
                pos = s0 + c * LANES + lax.broadcasted_iota(I32, (rows, LANES), 1)
                cnt = cnt + jnp.where(pred(kc, pos), 1.0, 0.0)
            return cnt
        cnt = lax.fori_loop(0, n_blocks, body, jnp.zeros((rows, LANES), F32))
        return jnp.sum(cnt, axis=-1, keepdims=True)

    def count_ge(cand):
        return count(lambda kc, pos: kc >= cand)

    thr = jnp.full((rows, 1), INT_MIN, I32)
    thr = jnp.where(count_ge(jnp.zeros((rows, 1), I32)) >= topk, 0, thr)

    def bit_body(i, thr):
        cand = thr | lax.shift_left(jnp.int32(1), 30 - i)
        return jnp.where(count_ge(cand) >= topk, cand, thr)

    thr = lax.fori_loop(0, 31, bit_body, thr)
    n_ge = count_ge(thr)
    n_gt = count(lambda kc, pos: kc > thr)
    need = topk - n_gt
    tie_ref[...] = jnp.full((rows, 1), 2 ** 30, I32)

    @pl.when(jnp.max(n_ge) > topk)
    def _():
        n_bits = max(1, int(math.ceil(math.log2(keys_ref.shape[1] + 1))))

        def cut_body(i, lo):
            cand = lo + lax.shift_left(jnp.int32(1), n_bits - 1 - i)
            f = count(lambda kc, pos: (kc == thr) & (pos <= cand))
            return jnp.where(f < need, cand, lo)

        lo = lax.fori_loop(0, n_bits, cut_body, jnp.full((rows, 1), -1, I32))
        tie_ref[...] = lo + 1

    cut = tie_ref[...]

    def write_body(kb, carry):
        s0 = pl.multiple_of(kb * blk, blk)
        keys = keys_ref[:, pl.ds(s0, blk)]
        pos = s0 + lax.broadcasted_iota(I32, (rows, blk), 1)
        sel = ((keys > thr) | ((keys == thr) & (pos <= cut))) & (pos <= q_pos)
        bias_ref[:, pl.ds(s0, blk)] = jnp.where(sel, 0.0, NEG_BIG)
        return carry

    lax.fori_loop(0, n_blocks, write_body, 0)


def _online_softmax_blocks(qs, ks, vs, bias, state):
    n = len(qs)
    ss = [lax.dot_general(qs[i], ks[i], (((1,), (1,)), ((), ())), preferred_element_type=F32) for i in range(n)]
    ms, ls, scales, ps = [], [], [], []
    for i in range(n):
        m, l, _ = state[i]
        s = ss[i] * (ATTN_SCALE * LOG2E) + bias
        m_new = jnp.maximum(m, jnp.max(s, axis=-1, keepdims=True))
        a = jnp.exp2(m - m_new)
        p = jnp.exp2(s - m_new)
        ms.append(m_new)
        ls.append(a * l + jnp.sum(p, axis=-1, keepdims=True))
        scales.append(a)
        ps.append(p.astype(BF16))
    return tuple((ms[i], ls[i], scales[i] * state[i][2] + jnp.dot(ps[i], vs[i], preferred_element_type=F32))
                 for i in range(n))


def _online_softmax_block(q, k, v, bias, m, l, acc):
    return _online_softmax_blocks([q], [k], [v], bias, ((m, l, acc),))[0]


def _dsa_prompt_kernel(q_ref, k_ref, v_ref, iq_ref, ik_ref, iw_ref, o_ref, keys_ref, bias_ref, tie_ref, iwb_ref,
                       *, bq, topk, hg):
    qi = pl.program_id(1)
    n_blocks = qi + 1
    q_pos = qi * bq + lax.broadcasted_iota(I32, (bq, 1), 0)

    @pl.when(pl.program_id(2) == 0)
    def _():
        iw = iw_ref[...]
        for hh in range(N_IDX_HEADS):
            iwb_ref[hh] = jnp.broadcast_to(iw[:, hh:hh + 1], (bq, LANES))

        def score_body(kb, carry):
            s0 = pl.multiple_of(kb * bq, bq)
            ik = ik_ref[pl.ds(s0, bq), :]
            sc = jnp.zeros((bq, bq), F32)
            for hh in range(N_IDX_HEADS):
                d = lax.dot_general(iq_ref[:, hh * IDX_DIM:(hh + 1) * IDX_DIM], ik, (((1,), (1,)), ((), ())),
                                    preferred_element_type=F32)
                sc = sc + jnp.maximum(d, 0.0) * jnp.concatenate([iwb_ref[hh]] * (bq // LANES), axis=-1)
            pos = s0 + lax.broadcasted_iota(I32, (bq, bq), 1)
            sc = jnp.where(pos <= q_pos, sc, -jnp.inf)
            keys_ref[:, pl.ds(s0, bq)] = _sortable_key(sc)
            return carry

        lax.fori_loop(0, n_blocks, score_body, 0)
        _topk_bias(keys_ref, bias_ref, tie_ref, n_blocks, bq, q_pos, topk)

    lanes = [slice(h * HEAD_DIM, (h + 1) * HEAD_DIM) for h in range(hg)]
    qs = [q_ref[:, ln] for ln in lanes]

    def body(kb, state):
        s0 = pl.multiple_of(kb * bq, bq)
        return _online_softmax_blocks(qs, [k_ref[pl.ds(s0, bq), ln] for ln in lanes],
                                      [v_ref[pl.ds(s0, bq), ln] for ln in lanes], bias_ref[:, pl.ds(s0, bq)], state)

    state = tuple((jnp.full((bq, 1), NEG_BIG, F32), jnp.zeros((bq, 1), F32), jnp.zeros((bq, HEAD_DIM), F32))
                  for _ in range(hg))
    state = lax.fori_loop(0, n_blocks, body, state)
    for h in range(hg):
        o_ref[:, lanes[h]] = (state[h][2] / state[h][1]).astype(o_ref.dtype)


def dsa_attention_prompt(q, k, v, iq, ik, iw, batch, t):
    bq = _pick(t, (256, 128))
    nq = t // bq
    hg = HEADS_PER_STEP
    topk = min(TOPK_MAX, t // 4)
    n_iq = N_IDX_HEADS * IDX_DIM
    return pl.pallas_call(
        functools.partial(_dsa_prompt_kernel, bq=bq, topk=topk, hg=hg),
        grid=(batch, nq, N_HEADS // hg),
        in_specs=[pl.BlockSpec((bq, hg * HEAD_DIM), lambda b, i, h: (b * nq + i, h)),
                  pl.BlockSpec((t, hg * HEAD_DIM), lambda b, i, h: (b, h)),
                  pl.BlockSpec((t, hg * HEAD_DIM), lambda b, i, h: (b, h)),
                  pl.BlockSpec((bq, n_iq), lambda b, i, h: (b * nq + i, 0)),
                  pl.BlockSpec((t, IDX_DIM), lambda b, i, h: (b, 0)),
                  pl.BlockSpec((bq, N_IDX_HEADS), lambda b, i, h: (b * nq + i, 0))],
        out_specs=pl.BlockSpec((bq, hg * HEAD_DIM), lambda b, i, h: (b * nq + i, h)),
        out_shape=jax.ShapeDtypeStruct((batch * t, D_MODEL), BF16),
        scratch_shapes=[pltpu.VMEM((bq, t), I32), pltpu.VMEM((bq, t), F32), pltpu.VMEM((bq, 1), I32),
                        pltpu.VMEM((N_IDX_HEADS, bq, LANES), F32)],
        compiler_params=_cparams(("parallel", "arbitrary", "arbitrary")),
        name="dsa_attention_prompt",
    )(q, k, v, iq, ik, iw)


def _idx_score_sample_kernel(pt_ref, iq_ref, iw_ref, *refs, t, pages):
    ik_refs, ikn_ref, o_ref = refs[:pages], refs[pages], refs[pages + 1]
    j = pl.program_id(1)

    def scores(ik):
        d = lax.dot_general(iq_ref[...], ik.astype(BF16), (((1,), (1,)), ((), ())), preferred_element_type=F32)
        wd = jnp.maximum(d, 0.0) * iw_ref[...]
        return jnp.sum(wd.reshape(t, N_IDX_HEADS, PAGE_SIZE), axis=1)

    @pl.when(j < pl.num_programs(1) - 1)
    def _():
        for r in range(pages):
            o_ref[:, r * PAGE_SIZE:(r + 1) * PAGE_SIZE] = scores(ik_refs[r][...])

    @pl.when(j == pl.num_programs(1) - 1)
    def _():
        o_ref[:, 0:PAGE_SIZE] = scores(ikn_ref[...])
        if pages > 1:
            o_ref[:, PAGE_SIZE:] = jnp.zeros((t, (pages - 1) * PAGE_SIZE), F32)


def idx_scores_sample(iq, iw, ik_new, cache_kidx, page_table, db, t):
    n_pages = page_table.shape[1]
    pages = _pick(n_pages, (IDX_PAGES_PER_STEP, 8, 4, 2, 1))
    n_steps = n_pages // pages
    rows = t * N_IDX_HEADS
    iq_rows = iq.reshape(db, rows, IDX_DIM)
    iw_rows = iw.reshape(db, rows, 1)
    ikn = jnp.pad(ik_new.reshape(db, t, IDX_DIM), ((0, 0), (0, PAGE_SIZE - t), (0, 0)))
    page_specs = [pl.BlockSpec((None, PAGE_SIZE, IDX_DIM),
                               lambda b, j, pt, r=r: (pt[b, jnp.minimum(j, n_steps - 1) * pages + r], 0, 0))
                  for r in range(pages)]
    grid_spec = pltpu.PrefetchScalarGridSpec(
        num_scalar_prefetch=1,
        grid=(db, n_steps + 1),
        in_specs=[pl.BlockSpec((None, rows, IDX_DIM), lambda b, j, pt: (b, 0, 0)),
                  pl.BlockSpec((None, rows, 1), lambda b, j, pt: (b, 0, 0))]
        + page_specs
        + [pl.BlockSpec((None, PAGE_SIZE, IDX_DIM), lambda b, j, pt: (b, 0, 0))],
        out_specs=pl.BlockSpec((None, t, pages * PAGE_SIZE), lambda b, j, pt: (b, 0, j)),
    )
    return pl.pallas_call(
        functools.partial(_idx_score_sample_kernel, t=t, pages=pages),
        grid_spec=grid_spec,
        out_shape=jax.ShapeDtypeStruct((db, t, (n_pages + pages) * PAGE_SIZE), F32),
        compiler_params=_cparams(("parallel", "arbitrary")),
        name="idx_scores_sample",
    )(page_table, iq_rows, iw_rows, *([cache_kidx] * pages), ikn)


def _topk_sample_kernel(s_ref, qpos_ref, bias_ref, keys_ref, tie_ref, *, topk, n_blocks):
    q_pos = qpos_ref[...]
    rows = q_pos.shape[0]

    def fill(kb, carry):
        s0 = pl.multiple_of(kb * LANES, LANES)
        pos = s0 + lax.broadcasted_iota(I32, (rows, LANES), 1)
        sc = jnp.where(pos <= q_pos, s_ref[:, pl.ds(s0, LANES)], -jnp.inf)
        keys_ref[:, pl.ds(s0, LANES)] = _sortable_key(sc)
        return carry

    lax.fori_loop(0, n_blocks, fill, 0)
    _topk_bias(keys_ref, bias_ref, tie_ref, n_blocks, LANES, q_pos, topk)


def topk_bias_sample(scores, q_pos, topk):
    r, l = scores.shape
    return pl.pallas_call(
        functools.partial(_topk_sample_kernel, topk=topk, n_blocks=l // LANES),
        out_shape=jax.ShapeDtypeStruct((r, l), F32),
        scratch_shapes=[pltpu.VMEM((r, l), I32), pltpu.VMEM((r, 1), I32)],
        compiler_params=pltpu.CompilerParams(vmem_limit_bytes=VMEM_LIMIT_BYTES),
        name="topk_bias_sample",
    )(scores, q_pos)


def _dsa_sample_kernel(pt_ref, q_ref, bias_ref, *refs, t, pages):
    n_cache = 2 * N_HEAD_GROUPS * pages
    cache_refs = refs[:n_cache]
    kn_ref, vn_ref, o_ref, m_ref, l_ref, acc_ref = refs[n_cache:]
    j = pl.program_id(1)
    rows = t * N_HEADS
    q = q_ref[...]

    @pl.when(j == 0)
    def _():
        m_ref[...] = jnp.full((rows, 1), NEG_BIG, F32)
        l_ref[...] = jnp.zeros((rows, 1), F32)
        acc_ref[...] = jnp.zeros((rows, D_MODEL), F32)

    def head_rows(bias):
        return jnp.broadcast_to(bias[:, None, :], (t, N_HEADS, bias.shape[1])).reshape(rows, bias.shape[1])

    @pl.when(j < pl.num_programs(1) - 1)
    def _():
        k = jnp.concatenate([_load_page(_page_refs(cache_refs, r)[0]) for r in range(pages)], axis=0)
        v = jnp.concatenate([_load_page(_page_refs(cache_refs, r)[1]) for r in range(pages)], axis=0)
        m_ref[...], l_ref[...], acc_ref[...] = _online_softmax_block(
            q, k, v, head_rows(bias_ref[...]), m_ref[...], l_ref[...], acc_ref[...])

    @pl.when(j == pl.num_programs(1) - 1)
    def _():
        m, l, acc = _online_softmax_block(q, kn_ref[...].astype(BF16), vn_ref[...].astype(BF16),
                                          head_rows(bias_ref[:, 0:PAGE_SIZE]), m_ref[...], l_ref[...], acc_ref[...])
        o_ref[...] = _extract_block_diag(acc / l, t)


def dsa_attention_sample(q, bias, k_new, v_new, cache_k, cache_v, page_table, db, t):
    n_pages = page_table.shape[1]
    pages = _pick(n_pages, (SAMPLE_PAGES_PER_STEP, 2, 1))
    n_steps = n_pages // pages
    assert bias.shape[2] >= (n_steps + 1) * pages * PAGE_SIZE
    rows = t * N_HEADS
    qbd = _block_diag_queries(q, db, t)

    def page_of(b, j, pt, r):
        return pt[b, jnp.minimum(j, n_steps - 1) * pages + r]

    grid_spec = pltpu.PrefetchScalarGridSpec(
        num_scalar_prefetch=1,
        grid=(db, n_steps + 1),
        in_specs=[pl.BlockSpec((None, rows, D_MODEL), lambda b, j, pt: (b, 0, 0)),
                  pl.BlockSpec((None, t, pages * PAGE_SIZE), lambda b, j, pt: (b, 0, j))]
        + _cache_specs(pages, page_of)
        + [pl.BlockSpec((None, PAGE_SIZE, D_MODEL), lambda b, j, pt: (b, 0, 0)),
           pl.BlockSpec((None, PAGE_SIZE, D_MODEL), lambda b, j, pt: (b, 0, 0))],
        out_specs=pl.BlockSpec((None, t, D_MODEL), lambda b, j, pt: (b, 0, 0)),
        scratch_shapes=[pltpu.VMEM((rows, 1), F32), pltpu.VMEM((rows, 1), F32), pltpu.VMEM((rows, D_MODEL), F32)],
    )
    out = pl.pallas_call(
        functools.partial(_dsa_sample_kernel, t=t, pages=pages),
        grid_spec=grid_spec,
        out_shape=jax.ShapeDtypeStruct((db, t, D_MODEL), F32),
        compiler_params=_cparams(("parallel", "arbitrary")),
        name="dsa_attention_sample",
    )(page_table, qbd, bias, *_cache_args(cache_k, cache_v, pages), _pad_new_page(k_new, db, t),
      _pad_new_page(v_new, db, t))
    return out.reshape(db * t, D_MODEL)


def _ffn(x, f_norm, wg, wu, wd):
    h = rmsnorm_rows(x, f_norm)
    return matmul_residual(ffn_up(h, wg, wu), wd, x)


def _sb_projections(x, attn_norm, w_qkv, q_norm, k_norm):
    h = rmsnorm_rows(x, attn_norm)
    (q,) = project(h, w_qkv, 0, D_MODEL, (BF16,), gain=q_norm)
    k32, k16 = project(h, w_qkv, D_MODEL, D_MODEL, (F32, BF16), gain=k_norm)
    v32, v16 = project(h, w_qkv, 2 * D_MODEL, D_MODEL, (F32, BF16))
    return q, k32, k16, v32, v16


def _dsa_projections(x, attn_norm, w_in, w_small, q_norm, k_norm, rope_qk, rope_idx):
    h = rmsnorm_rows(x, attn_norm)
    (q,) = project(h, w_in, 0, D_MODEL, (BF16,), gain=q_norm, rope=rope_qk, rope_shift=ROT_DIM // 2)
    k32, k16 = project(h, w_in, D_MODEL, D_MODEL, (F32, BF16), gain=k_norm, rope=rope_qk, rope_shift=ROT_DIM // 2)
    v32, v16 = project(h, w_in, 2 * D_MODEL, D_MODEL, (F32, BF16))
    (iq,) = project(h, w_in, 3 * D_MODEL, N_IDX_HEADS * IDX_DIM, (BF16,), rope=rope_idx, rope_shift=IDX_ROT_DIM // 2)
    (tail_rot,) = project(h, w_small, 0, LANES, (F32,), rope=rope_idx, rope_shift=IDX_ROT_DIM // 2)
    (tail_raw,) = project(h, w_small, 0, LANES, (F32,))
    ik32 = tail_rot[:, :IDX_DIM]
    iw = tail_raw[:, IDX_DIM:IDX_DIM + N_IDX_HEADS]
    return q, k32, k16, v32, v16, iq, ik32, iw


def kernel(x_prompt, x_sample, cache_k_0, cache_v_0, state_pool_1, cache_k_2, cache_v_2, cache_kidx_2, cache_k_3, cache_v_3, page_table, attn_norm_0, w_qkv_0, q_norm_0, k_norm_0, w_o_0, ffn_norm_0, w_gate_0, w_up_0, w_down_0, attn_norm_1, w_pool_1, pool_scale_1, ffn_norm_1, w_gate_1, w_up_1, w_down_1, attn_norm_2, w_in_2, q_norm_2, k_norm_2, w_o_2, ffn_norm_2, w_gate_2, w_up_2, w_down_2, attn_norm_3, w_qkv_3, q_norm_3, k_norm_3, w_o_3, ffn_norm_3, w_gate_3, w_up_3, w_down_3):
    batch, t, d = x_prompt.shape
    db, dt, _ = x_sample.shape
    n_phys = cache_k_0.shape[0]
    n_pages = page_table.shape[1]
    past = n_pages * PAGE_SIZE
    mp, ms = batch * t, db * dt
    xp = x_prompt.reshape(mp, d)
    xs = x_sample.reshape(ms, d)

    def bf(w):
        return w.astype(BF16)

    def heads(a, b_, t_):
        return a.reshape(b_, t_, N_HEADS, HEAD_DIM)

    def sb_layer(xp, xs, cache_k, cache_v, attn_norm, w_qkv, q_norm, k_norm, w_o):
        w_qkv = bf(w_qkv)
        q, k32, k16, v32, v16 = _sb_projections(xp, attn_norm, w_qkv, q_norm, k_norm)
        xp = matmul_residual(sb_attention_prompt(q, k16, v16, batch, t), w_o, xp)
        qs, ks32, _, vs32, _ = _sb_projections(xs, attn_norm, w_qkv, q_norm, k_norm)
        o_s = sb_attention_sample(qs, ks32, vs32, cache_k, cache_v, page_table, db, dt)
        xs = matmul_residual(o_s.astype(BF16), w_o, xs)
        return xp, xs, (heads(k32, batch, t), heads(v32, batch, t)), (heads(ks32, db, dt), heads(vs32, db, dt))

    xp, xs, (k0_p, v0_p), (k0_s, v0_s) = sb_layer(xp, xs, cache_k_0, cache_v_0, attn_norm_0, w_qkv_0, q_norm_0,
                                                   k_norm_0, w_o_0)
    xp = _ffn(xp, ffn_norm_0, w_gate_0, w_up_0, w_down_0)
    xs = _ffn(xs, ffn_norm_0, w_gate_0, w_up_0, w_down_0)

    w_pool = bf(w_pool_1)
    pool1_p = rmsnorm_rows(xp.reshape(batch, t, d)[:, t - POOL_HALO:].reshape(batch * POOL_HALO, d), attn_norm_1,
                           F32).reshape(batch, POOL_HALO, d)[:, POOL_HALO - POOL_STATE:]
    hs_new = rmsnorm_rows(xs, attn_norm_1, F32)
    pool1_s = jnp.concatenate([state_pool_1, hs_new.reshape(db, dt, d)], axis=1)[:, -POOL_STATE:]
    xp = pool_layer(xp, xp, attn_norm_1, w_pool, pool_scale_1, seq_len=t, halo_is_state=False, pos0=0)
    rows_s = 8
    xs_pad = jnp.pad(xs.reshape(db, dt, d), ((0, 0), (0, rows_s - dt), (0, 0))).reshape(db * rows_s, d)
    state_pad = jnp.pad(state_pool_1, ((0, 0), (POOL_HALO - POOL_STATE, 0), (0, 0))).reshape(db * POOL_HALO, d)
    xs = pool_layer(xs_pad, state_pad, attn_norm_1, w_pool, pool_scale_1, seq_len=rows_s, halo_is_state=True,
                    pos0=PAST_LEN).reshape(db, rows_s, d)[:, :dt].reshape(ms, d)
    xp = _ffn(xp, ffn_norm_1, w_gate_1, w_up_1, w_down_1)
    xs = _ffn(xs, ffn_norm_1, w_gate_1, w_up_1, w_down_1)

    w_in, w_o2 = bf(w_in_2), w_o_2
    n_main = 3 * D_MODEL + N_IDX_HEADS * IDX_DIM
    w_small = jnp.pad(w_in[:, n_main:], ((0, 0), (0, LANES - (w_in.shape[1] - n_main))))
    pos_p = jnp.arange(t, dtype=I32)
    pos_s = jnp.tile(past + jnp.arange(dt, dtype=I32), db)
    q, k32, k16, v32, v16, iq, ik32, iw = _dsa_projections(
        xp, attn_norm_2, w_in, w_small, q_norm_2, k_norm_2,
        rope_tables(pos_p, ROT_DIM, HEAD_DIM), rope_tables(pos_p, IDX_ROT_DIM, IDX_DIM))
    iw_scaled = iw * (IDX_W_SCALE * IDX_SCALE)
    o_p = dsa_attention_prompt(q, k16, v16, iq, ik32.astype(BF16), iw_scaled, batch, t)
    k2_p, v2_p, kidx2_p = heads(k32, batch, t), heads(v32, batch, t), ik32.reshape(batch, t, IDX_DIM)
    xp = matmul_residual(o_p, w_o2, xp)

    qs, ks32, _, vs32, _, iqs, iks32, iws = _dsa_projections(
        xs, attn_norm_2, w_in, w_small, q_norm_2, k_norm_2,
        rope_tables(pos_s, ROT_DIM, HEAD_DIM), rope_tables(pos_s, IDX_ROT_DIM, IDX_DIM))
    scores = idx_scores_sample(iqs, iws * (IDX_W_SCALE * IDX_SCALE), iks32, cache_kidx_2, page_table, db, dt)
    topk_s = min(TOPK_MAX, (past + dt) // 4)
    bias_s = topk_bias_sample(scores.reshape(ms, -1), pos_s.reshape(ms, 1), topk_s).reshape(db, dt, -1)
    o_s = dsa_attention_sample(qs, bias_s, ks32, vs32, cache_k_2, cache_v_2, page_table, db, dt)
    k2_s, v2_s, kidx2_s = heads(ks32, db, dt), heads(vs32, db, dt), iks32.reshape(db, dt, IDX_DIM)
    xs = matmul_residual(o_s.astype(BF16), w_o2, xs)
    xp = _ffn(xp, ffn_norm_2, w_gate_2, w_up_2, w_down_2)
    xs = _ffn(xs, ffn_norm_2, w_gate_2, w_up_2, w_down_2)

    xp, xs, (k3_p, v3_p), (k3_s, v3_s) = sb_layer(xp, xs, cache_k_3, cache_v_3, attn_norm_3, w_qkv_3, q_norm_3,
                                                   k_norm_3, w_o_3)
    xp = _ffn(xp, ffn_norm_3, w_gate_3, w_up_3, w_down_3)
    xs = _ffn(xs, ffn_norm_3, w_gate_3, w_up_3, w_down_3)

    return (xp.reshape(batch, t, d), xs.reshape(db, dt, d), k0_p, v0_p, k0_s, v0_s, pool1_p, pool1_s,
            k2_p, v2_p, kidx2_p, k2_s, v2_s, kidx2_s, k3_p, v3_p, k3_s, v3_s)
```

```python
import functools
import math

import jax
import jax.numpy as jnp
from jax import lax
from jax.experimental import pallas as pl
from jax.experimental.pallas import tpu as pltpu

F32 = jnp.float32
BF16 = jnp.bfloat16
I32 = jnp.int32

D_MODEL = 2048
N_HEADS = 16
HEAD_DIM = 128
ROT_DIM = 32
ROPE_THETA = 500000.0
POOL_WINDOWS = (2, 4, 8, 16)
POOL_GROUP_DIM = 512
POOL_STATE = 15
N_IDX_HEADS = 16
IDX_DIM = 64
IDX_ROT_DIM = 16
TOPK_MAX = 256
PAGE_SIZE = 128
PAST_LEN = 16384
RMS_EPS = 1e-6
ATTN_SCALE = HEAD_DIM ** -0.5
IDX_SCALE = IDX_DIM ** -0.5
IDX_W_SCALE = N_IDX_HEADS ** -0.5
LOG2E = 1.4426950408889634

HEADS_PER_STEP = 4
SAMPLE_PAGES_PER_STEP = 4
IDX_PAGES_PER_STEP = 16
HEAD_GROUP = 8
PROJ_ROW_CHUNK = 256

LANES = 128
POOL_HALO = 16
VMEM_LIMIT_BYTES = 56 * 1024 * 1024
NEG_BIG = -1e30
INT_MIN = -2 ** 31
SB_DEAD_LOG2 = -160.0


def _cparams(sem):
    return pltpu.CompilerParams(dimension_semantics=sem, vmem_limit_bytes=VMEM_LIMIT_BYTES)


def _pick(n, cands):
    for c in cands:
        if n % c == 0:
            return c
    return n


def _rmsnorm_kernel(x_ref, g_ref, o_ref):
    x = x_ref[...]
    ms = jnp.mean(x * x, axis=-1, keepdims=True)
    o_ref[...] = (x * lax.rsqrt(ms + RMS_EPS) * g_ref[...]).astype(o_ref.dtype)


def rmsnorm_rows(x, g, out_dtype=BF16):
    m, d = x.shape
    bm = _pick(m, (512, 256, 128, 64, 32, 16, 8))
    return pl.pallas_call(
        _rmsnorm_kernel,
        grid=(m // bm,),
        in_specs=[pl.BlockSpec((bm, d), lambda i: (i, 0)),
                  pl.BlockSpec((1, d), lambda i: (0, 0))],
        out_specs=pl.BlockSpec((bm, d), lambda i: (i, 0)),
        out_shape=jax.ShapeDtypeStruct((m, d), out_dtype),
        compiler_params=_cparams(("parallel",)),
        name="rmsnorm",
    )(x, g.reshape(1, d))


def _proj_kernel(*refs, norm, rope_shift, scale, n_out, row_chunk):
    x_ref, w_ref = refs[0], refs[1]
    pos = 2
    g_ref = None
    if norm:
        g_ref = refs[pos]
        pos += 1
    tabs = None
    if rope_shift:
        tabs = refs[pos:pos + 3]
        pos += 3
    out_refs = refs[pos:pos + n_out]
    w_bf = refs[pos + n_out]
    bm, bn = out_refs[0].shape

    @pl.when(pl.program_id(1) == 0)
    def _():
        w_bf[...] = w_ref[...].astype(BF16)

    w = w_bf[...]
    chunks = [slice(r, r + row_chunk) for r in range(0, bm, row_chunk)]
    ys = [jnp.dot(x_ref[rows, :], w, preferred_element_type=F32) for rows in chunks]
    for rows, y in zip(chunks, ys):
        parts = []
        for h in range(bn // LANES):
            yh = y[:, h * LANES:(h + 1) * LANES]
            if norm:
                ms = jnp.mean(yh * yh, axis=-1, keepdims=True)
                yh = yh * lax.rsqrt(ms + RMS_EPS) * g_ref[:, h * LANES:(h + 1) * LANES]
            if rope_shift:
                c, sa, sb = (tab[rows, :] for tab in tabs)
                yh = yh * c + pltpu.roll(yh, rope_shift, 1) * sa + pltpu.roll(yh, LANES - rope_shift, 1) * sb
            if scale is not None:
                yh = yh * scale
            parts.append(yh)
        y = jnp.concatenate(parts, axis=-1)
        for o_ref in out_refs:
            o_ref[rows, :] = y.astype(o_ref.dtype)


def project(h, w, col0, n, out_dtypes, *, gain=None, rope=None, rope_shift=0, scale=None, bn=512):
    m, k = h.shape
    bm = _pick(m, (1024, 512, 256, 128, 64, 32, 16))
    bn = min(bn, n)
    assert n % bn == 0 and col0 % bn == 0 and bn % LANES == 0
    off = col0 // bn
    in_specs = [pl.BlockSpec((bm, k), lambda j, i: (i, 0)),
                pl.BlockSpec((k, bn), lambda j, i: (0, j + off))]
    args = [h, w]
    if gain is not None:
        in_specs.append(pl.BlockSpec((1, bn), lambda j, i: (0, 0)))
        args.append(jnp.tile(gain.reshape(1, -1), (1, bn // gain.shape[-1])))
    if rope is not None:
        period = rope[0].shape[0] // bm
        for tab in rope:
            in_specs.append(pl.BlockSpec((bm, LANES), lambda j, i: (i % period, 0)))
            args.append(tab)
    outs = pl.pallas_call(
        functools.partial(_proj_kernel, norm=gain is not None, rope_shift=rope_shift if rope is not None else 0,
                          scale=scale, n_out=len(out_dtypes), row_chunk=_pick(bm, (PROJ_ROW_CHUNK,))),
        grid=(n // bn, m // bm),
        in_specs=in_specs,
        out_specs=[pl.BlockSpec((bm, bn), lambda j, i: (i, j)) for _ in out_dtypes],
        out_shape=[jax.ShapeDtypeStruct((m, n), dt) for dt in out_dtypes],
        scratch_shapes=[pltpu.VMEM((k, bn), BF16)],
        compiler_params=_cparams(("arbitrary", "arbitrary")),
        name="project",
    )(*args)
    return outs


def rope_tables(pos, rot_dim, head_dim):
    half = rot_dim // 2
    inv_freq = ROPE_THETA ** (-jnp.arange(half, dtype=F32) / half)
    ang = pos.astype(F32)[:, None] * inv_freq[None, :]
    cos, sin = jnp.cos(ang), jnp.sin(ang)
    p = pos.shape[0]
    ones = jnp.ones((p, head_dim - rot_dim), F32)
    zeros_h = jnp.zeros((p, half), F32)
    zeros_r = jnp.zeros((p, head_dim - rot_dim), F32)
    c = jnp.concatenate([cos, cos, ones], axis=-1)
    sa = jnp.concatenate([zeros_h, sin, zeros_r], axis=-1)
    sb = jnp.concatenate([-sin, zeros_h, zeros_r], axis=-1)
    reps = LANES // head_dim
    return tuple(jnp.tile(t, (1, reps)) for t in (c, sa, sb))


def _mm_res_kernel(a_ref, w_ref, r_ref, o_ref, w_bf):
    @pl.when(pl.program_id(1) == 0)
    def _():
        w_bf[...] = w_ref[...].astype(BF16)

    o_ref[...] = r_ref[...] + jnp.dot(a_ref[...], w_bf[...], preferred_element_type=F32)


def matmul_residual(a, w, res):
    m, k = a.shape
    n = w.shape[1]
    bm = _pick(m, (1024 if k <= D_MODEL else 512, 512, 256, 128, 64, 32, 16))
    bn = _pick(n, (512, 256, 128))
    return pl.pallas_call(
        _mm_res_kernel,
        grid=(n // bn, m // bm),
        in_specs=[pl.BlockSpec((bm, k), lambda j, i: (i, 0)),
                  pl.BlockSpec((k, bn), lambda j, i: (0, j)),
                  pl.BlockSpec((bm, bn), lambda j, i: (i, j))],
        out_specs=pl.BlockSpec((bm, bn), lambda j, i: (i, j)),
        out_shape=jax.ShapeDtypeStruct((m, n), F32),
        scratch_shapes=[pltpu.VMEM((k, bn), BF16)],
        compiler_params=_cparams(("arbitrary", "arbitrary")),
        name="matmul_residual",
    )(a, w, res)


def _ffn_up_kernel(x_ref, wg_ref, wu_ref, o_ref, wg_bf, wu_bf):
    @pl.when(pl.program_id(1) == 0)
    def _():
        wg_bf[...] = wg_ref[...].astype(BF16)
        wu_bf[...] = wu_ref[...].astype(BF16)

    x = x_ref[...]
    g = jnp.dot(x, wg_bf[...], preferred_element_type=F32)
    u = jnp.dot(x, wu_bf[...], preferred_element_type=F32)
    o_ref[...] = (g * (1.0 / (1.0 + jnp.exp(-g))) * u).astype(o_ref.dtype)


def ffn_up(h, wg, wu):
    m, k = h.shape
    n = wg.shape[1]
    bm = _pick(m, (1024, 512, 256, 128, 64, 32, 16))
    bn = _pick(n, (512, 256, 128))
    return pl.pallas_call(
        _ffn_up_kernel,
        grid=(n // bn, m // bm),
        in_specs=[pl.BlockSpec((bm, k), lambda j, i: (i, 0)),
                  pl.BlockSpec((k, bn), lambda j, i: (0, j)),
                  pl.BlockSpec((k, bn), lambda j, i: (0, j))],
        out_specs=pl.BlockSpec((bm, bn), lambda j, i: (i, j)),
        out_shape=jax.ShapeDtypeStruct((m, n), BF16),
        scratch_shapes=[pltpu.VMEM((k, bn), BF16), pltpu.VMEM((k, bn), BF16)],
        compiler_params=_cparams(("arbitrary", "arbitrary")),
        name="ffn_up",
    )(h, wg, wu)


def _sb_blocks(qs, ks, vs, u2, state, vis):
    n = len(qs)
    zs = [lax.dot_general(qs[i], ks[i], (((1,), (1,)), ((), ())), preferred_element_type=F32) for i in range(n)]
    log_betas, sums, hls = [], [], []
    for z in zs:
        z = z * (ATTN_SCALE * LOG2E)
        neg_abs = pltpu.bitcast(pltpu.bitcast(z, I32) | jnp.int32(INT_MIN), F32)
        log_beta = jnp.minimum(z, 0.0) - jnp.log(1.0 + jnp.exp2(neg_abs)) * LOG2E
        log_keep = log_beta - z
        if vis is not None:
            log_keep = jnp.where(vis, log_keep, 0.0)
        hi = log_keep.astype(BF16)
        lo = (log_keep - hi.astype(F32)).astype(BF16)
        log_betas.append(log_beta)
        sums.append(jnp.sum(log_keep, axis=-1, keepdims=True))
        hls.append(jnp.concatenate([hi, lo], axis=-1))
    tails = [jnp.dot(hl, u2, preferred_element_type=F32) for hl in hls]
    ws = []
    for i in range(n):
        w = jnp.exp2(log_betas[i] + (tails[i] + state[i][0]))
        if vis is not None:
            w = jnp.where(vis, w, 0.0)
        ws.append(w.astype(BF16))
    return tuple((state[i][0] + sums[i], state[i][1] + jnp.dot(ws[i], vs[i], preferred_element_type=F32))
                 for i in range(n))


def _sb_block(q, k, v, u2, carry, acc, vis):
    return _sb_blocks([q], [k], [v], u2, ((carry, acc),), vis)[0]


def _sb_any_live(state):
    live = state[0][0]
    for carry, _ in state[1:]:
        live = jnp.maximum(live, carry)
    return (jnp.max(live) >= SB_DEAD_LOG2).astype(I32)


def _suffix_matrix2(c):
    row = lax.broadcasted_iota(I32, (2 * c, c), 0)
    col = lax.broadcasted_iota(I32, (2 * c, c), 1)
    return (jnp.where(row < c, row, row - c) > col).astype(BF16)


def _sb_prompt_kernel(q_ref, k_ref, v_ref, o_ref, *, bq, hg):
    qi = pl.program_id(2)
    u2 = _suffix_matrix2(bq)
    row = lax.broadcasted_iota(I32, (bq, bq), 0)
    col = lax.broadcasted_iota(I32, (bq, bq), 1)
    lanes = [slice(h * HEAD_DIM, (h + 1) * HEAD_DIM) for h in range(hg)]
    qs = [q_ref[:, ln] for ln in lanes]

    def blocks(s0, state, vis):
        return _sb_blocks(qs, [k_ref[pl.ds(s0, bq), ln] for ln in lanes], [v_ref[pl.ds(s0, bq), ln] for ln in lanes],
                          u2, state, vis)

    state = tuple((jnp.zeros((bq, 1), F32), jnp.zeros((bq, HEAD_DIM), F32)) for _ in range(hg))
    state = blocks(pl.multiple_of(qi * bq, bq), state, col < row)

    def step(c):
        i, _, st = c
        st = blocks(pl.multiple_of((qi - 1 - i) * bq, bq), st, None)
        return i + 1, _sb_any_live(st), st

    _, _, state = lax.while_loop(lambda c: (c[0] < qi) & (c[1] > 0), step, (jnp.int32(0), _sb_any_live(state), state))
    for h in range(hg):
        o_ref[:, lanes[h]] = state[h][1].astype(o_ref.dtype)


def sb_attention_prompt(q, k, v, batch, t):
    bq = _pick(t, (256, 128))
    nq = t // bq
    hg = HEADS_PER_STEP
    return pl.pallas_call(
        functools.partial(_sb_prompt_kernel, bq=bq, hg=hg),
        grid=(batch, N_HEADS // hg, nq),
        in_specs=[pl.BlockSpec((bq, hg * HEAD_DIM), lambda b, h, i: (b * nq + i, h)),
                  pl.BlockSpec((t, hg * HEAD_DIM), lambda b, h, i: (b, h)),
                  pl.BlockSpec((t, hg * HEAD_DIM), lambda b, h, i: (b, h))],
        out_specs=pl.BlockSpec((bq, hg * HEAD_DIM), lambda b, h, i: (b * nq + i, h)),
        out_shape=jax.ShapeDtypeStruct((batch * t, D_MODEL), BF16),
        compiler_params=_cparams(("parallel", "parallel", "arbitrary")),
        name="sb_attention_prompt",
    )(q, k, v)


def _block_diag_queries(q, db, t):
    q5 = q.reshape(db, t, N_HEADS, 1, HEAD_DIM)
    eye = jnp.eye(N_HEADS, dtype=jnp.bool_).reshape(1, 1, N_HEADS, N_HEADS, 1)
    return jnp.where(eye, q5, jnp.zeros_like(q5)).reshape(db, t * N_HEADS, D_MODEL)


def _extract_block_diag(acc, t):
    r = lax.broadcasted_iota(I32, acc.shape, 0)
    c = lax.broadcasted_iota(I32, acc.shape, 1)
    keep = (r % N_HEADS) == (c // HEAD_DIM)
    return jnp.sum(jnp.where(keep, acc, 0.0).reshape(t, N_HEADS, acc.shape[1]), axis=1)


N_HEAD_GROUPS = N_HEADS // HEAD_GROUP


def _load_page(group_refs):
    parts = []
    for ref in group_refs:
        rows = ref.reshape(PAGE_SIZE * HEAD_GROUP, HEAD_DIM)
        parts += [rows[pl.ds(h, PAGE_SIZE, stride=HEAD_GROUP), :] for h in range(HEAD_GROUP)]
    return jnp.concatenate(parts, axis=-1).astype(BF16)


def _page_refs(cache_refs, r):
    base = 2 * N_HEAD_GROUPS * r
    return cache_refs[base:base + N_HEAD_GROUPS], cache_refs[base + N_HEAD_GROUPS:base + 2 * N_HEAD_GROUPS]


def _cache_specs(pages, page_of):
    specs = []
    for r in range(pages):
        for _ in range(2):
            for g in range(N_HEAD_GROUPS):
                specs.append(pl.BlockSpec((None, PAGE_SIZE, HEAD_GROUP, HEAD_DIM),
                                          lambda b, j, pt, r=r, g=g: (page_of(b, j, pt, r), 0, g, 0)))
    return specs


def _cache_args(cache_k, cache_v, pages):
    return ([cache_k] * N_HEAD_GROUPS + [cache_v] * N_HEAD_GROUPS) * pages


def _load_flat_page(page_ref):
    return jnp.concatenate([page_ref[pl.ds(h, PAGE_SIZE, stride=N_HEADS), :] for h in range(N_HEADS)],
                           axis=-1).astype(BF16)


def _sb_sample_kernel(pt_ref, q_ref, kn_ref, vn_ref, ck_hbm, cv_hbm, o_ref, kbuf, vbuf, sem, carry_ref, acc_ref,
                      *, t, n_pages):
    b = pl.program_id(0)
    rows = t * N_HEADS
    q = q_ref[...]

    def page_copies(i, slot):
        page = pt_ref[b, n_pages - 1 - i]
        return (pltpu.make_async_copy(ck_hbm.at[page], kbuf.at[slot], sem.at[0, slot]),
                pltpu.make_async_copy(cv_hbm.at[page], vbuf.at[slot], sem.at[1, slot]))

    def start(i, slot):
        for cp in page_copies(i, slot):
            cp.start()

    def wait(i, slot):
        for cp in page_copies(i, slot):
            cp.wait()

    start(0, 0)
    r = lax.broadcasted_iota(I32, (rows, PAGE_SIZE), 0)
    c = lax.broadcasted_iota(I32, (rows, PAGE_SIZE), 1)
    carry, acc = _sb_block(q, kn_ref[...].astype(BF16), vn_ref[...].astype(BF16), _suffix_matrix2(PAGE_SIZE),
                           jnp.zeros((rows, 1), F32), jnp.zeros((rows, D_MODEL), F32), c < r // N_HEADS)
    carry_ref[...] = carry
    acc_ref[...] = acc
    u2 = _suffix_matrix2(PAGE_SIZE)

    def body(c):
        i, _ = c
        slot = lax.rem(i, 2)
        wait(i, slot)

        @pl.when(i + 1 < n_pages)
        def _():
            start(i + 1, 1 - slot)

        carry, acc = _sb_block(q, _load_flat_page(kbuf.at[slot]), _load_flat_page(vbuf.at[slot]), u2,
                               carry_ref[...], acc_ref[...], None)
        carry_ref[...] = carry
        acc_ref[...] = acc
        return i + 1, _sb_any_live(((carry, acc),))

    i_end, _ = lax.while_loop(lambda c: (c[0] < n_pages) & (c[1] > 0), body,
                              (jnp.int32(0), _sb_any_live(((carry, acc),))))

    @pl.when(i_end < n_pages)
    def _():
        wait(i_end, lax.rem(i_end, 2))

    o_ref[...] = _extract_block_diag(acc_ref[...], t)


def _pad_new_page(x, db, t):
    return jnp.pad(x.reshape(db, t, D_MODEL), ((0, 0), (0, PAGE_SIZE - t), (0, 0)))


def sb_attention_sample(q, k_new, v_new, cache_k, cache_v, page_table, db, t):
    n_phys = cache_k.shape[0]
    n_pages = page_table.shape[1]
    rows = t * N_HEADS
    qbd = _block_diag_queries(q, db, t)
    page_rows = PAGE_SIZE * N_HEADS

    def flat(cache):
        return cache.reshape(n_phys, page_rows, HEAD_DIM)

    grid_spec = pltpu.PrefetchScalarGridSpec(
        num_scalar_prefetch=1,
        grid=(db,),
        in_specs=[pl.BlockSpec((None, rows, D_MODEL), lambda b, pt: (b, 0, 0)),
                  pl.BlockSpec((None, PAGE_SIZE, D_MODEL), lambda b, pt: (b, 0, 0)),
                  pl.BlockSpec((None, PAGE_SIZE, D_MODEL), lambda b, pt: (b, 0, 0)),
                  pl.BlockSpec(memory_space=pl.ANY),
                  pl.BlockSpec(memory_space=pl.ANY)],
        out_specs=pl.BlockSpec((None, t, D_MODEL), lambda b, pt: (b, 0, 0)),
        scratch_shapes=[pltpu.VMEM((2, page_rows, HEAD_DIM), F32), pltpu.VMEM((2, page_rows, HEAD_DIM), F32),
                        pltpu.SemaphoreType.DMA((2, 2)),
                        pltpu.VMEM((rows, 1), F32), pltpu.VMEM((rows, D_MODEL), F32)],
    )
    out = pl.pallas_call(
        functools.partial(_sb_sample_kernel, t=t, n_pages=n_pages),
        grid_spec=grid_spec,
        out_shape=jax.ShapeDtypeStruct((db, t, D_MODEL), F32),
        compiler_params=_cparams(("arbitrary",)),
        name="sb_attention_sample",
    )(page_table, qbd, _pad_new_page(k_new, db, t), _pad_new_page(v_new, db, t), flat(cache_k), flat(cache_v))
    return out.reshape(db * t, D_MODEL)


def _pool_kernel(x_ref, halo_ref, g_ref, w_ref, ps_ref, o_ref, ext_ref, *, bm, blocks_per_seq, halo_is_state, pos0):
    i = pl.program_id(0)
    g = g_ref[...]

    def norm(x):
        ms = jnp.mean(x * x, axis=-1, keepdims=True)
        return x * lax.rsqrt(ms + RMS_EPS) * g

    x = x_ref[...]
    h = norm(x)
    if halo_is_state:
        halo = halo_ref[...]
    else:
        halo = jnp.where(i % blocks_per_seq == 0, 0.0, norm(halo_ref[...]))
    ext_ref[0:POOL_HALO, :] = halo
    ext_ref[POOL_HALO:POOL_HALO + bm, :] = h
    t_in_seq = (i % blocks_per_seq) * bm + lax.broadcasted_iota(I32, (bm, 1), 0) + pos0
    for gi, win in enumerate(POOL_WINDOWS):
        c0 = gi * POOL_GROUP_DIM
        c1 = c0 + POOL_GROUP_DIM
        wsum = h[:, c0:c1]
        for d in range(1, win):
            wsum = wsum + ext_ref[POOL_HALO - d:POOL_HALO - d + bm, c0:c1]
        count = jnp.minimum(t_in_seq + 1, win).astype(F32)
        pooled = wsum / count - h[:, c0:c1]
        y = jnp.dot(pooled.astype(BF16), w_ref[gi], preferred_element_type=F32)
        o_ref[:, c0:c1] = x[:, c0:c1] + y * ps_ref[:, c0:c1]


def pool_layer(x, halo_src, gain, w_pool, pool_scale, *, seq_len, halo_is_state, pos0):
    m, d = x.shape
    bm = _pick(seq_len, (256, 128, 64, 32, 16, 8))
    blocks_per_seq = seq_len // bm
    if halo_is_state:
        halo_spec = pl.BlockSpec((POOL_HALO, d), lambda i: (i, 0))
    else:
        per = bm // POOL_HALO
        halo_spec = pl.BlockSpec((POOL_HALO, d), lambda i: (jnp.maximum(i * per - 1, 0), 0))
    return pl.pallas_call(
        functools.partial(_pool_kernel, bm=bm, blocks_per_seq=blocks_per_seq, halo_is_state=halo_is_state, pos0=pos0),
        grid=(m // bm,),
        in_specs=[pl.BlockSpec((bm, d), lambda i: (i, 0)),
                  halo_spec,
                  pl.BlockSpec((1, d), lambda i: (0, 0)),
                  pl.BlockSpec(w_pool.shape, lambda i: (0, 0, 0)),
                  pl.BlockSpec((1, d), lambda i: (0, 0))],
        out_specs=pl.BlockSpec((bm, d), lambda i: (i, 0)),
        out_shape=jax.ShapeDtypeStruct((m, d), F32),
        scratch_shapes=[pltpu.VMEM((POOL_HALO + bm, d), F32)],
        compiler_params=_cparams(("parallel",)),
        name="pool_layer",
    )(x, halo_src, gain.reshape(1, d), w_pool, pool_scale.reshape(1, d))


def _sortable_key(x):
    bits = pltpu.bitcast(x, I32)
    return jnp.where(bits < 0, bits ^ jnp.int32(0x7FFFFFFF), bits)


def _lane_chunks(x):
    return [x[:, c * LANES:(c + 1) * LANES] for c in range(x.shape[1] // LANES)]


def _topk_bias(keys_ref, bias_ref, tie_ref, n_blocks, blk, q_pos, topk):
    rows = q_pos.shape[0]

    def count(pred):
        def body(kb, cnt):
            s0 = pl.multiple_of(kb * blk, blk)
            keys = keys_ref[:, pl.ds(s0, blk)]
            for c, kc in enumerate(_lane_chunks(keys)):
                pos = s0 + c * LANES + lax.broadcasted_iota(I32, (rows, LANES), 1)
                cnt = cnt + jnp.where(pred(kc, pos), 1.0, 0.0)
            return cnt
        cnt = lax.fori_loop(0, n_blocks, body, jnp.zeros((rows, LANES), F32))
        return jnp.sum(cnt, axis=-1, keepdims=True)

    def count_ge(cand):
        return count(lambda kc, pos: kc >= cand)

    thr = jnp.full((rows, 1), INT_MIN, I32)
    thr = jnp.where(count_ge(jnp.zeros((rows, 1), I32)) >= topk, 0, thr)

    def bit_body(i, thr):
        cand = thr | lax.shift_left(jnp.int32(1), 30 - i)
        return jnp.where(count_ge(cand) >= topk, cand, thr)

    thr = lax.fori_loop(0, 31, bit_body, thr)
    n_ge = count_ge(thr)
    n_gt = count(lambda kc, pos: kc > thr)
    need = topk - n_gt
    tie_ref[...] = jnp.full((rows, 1), 2 ** 30, I32)

    @pl.when(jnp.max(n_ge) > topk)
    def _():
        n_bits = max(1, int(math.ceil(math.log2(keys_ref.shape[1] + 1))))

        def cut_body(i, lo):
            cand = lo + lax.shift_left(jnp.int32(1), n_bits - 1 - i)
            f = count(lambda kc, pos: (kc == thr) & (pos <= cand))
            return jnp.where(f < need, cand, lo)

        lo = lax.fori_loop(0, n_bits, cut_body, jnp.full((rows, 1), -1, I32))
        tie_ref[...] = lo + 1

    cut = tie_ref[...]

    def write_body(kb, carry):
        s0 = pl.multiple_of(kb * blk, blk)
        keys = keys_ref[:, pl.ds(s0, blk)]
        pos = s0 + lax.broadcasted_iota(I32, (rows, blk), 1)
        sel = ((keys > thr) | ((keys == thr) & (pos <= cut))) & (pos <= q_pos)
        bias_ref[:, pl.ds(s0, blk)] = jnp.where(sel, 0.0, NEG_BIG)
        return carry

    lax.fori_loop(0, n_blocks, write_body, 0)


def _online_softmax_blocks(qs, ks, vs, bias, state):
    n = len(qs)
    ss = [lax.dot_general(qs[i], ks[i], (((1,), (1,)), ((), ())), preferred_element_type=F32) for i in range(n)]
    ms, ls, scales, ps = [], [], [], []
    for i in range(n):
        m, l, _ = state[i]
        s = ss[i] * (ATTN_SCALE * LOG2E) + bias
        m_new = jnp.maximum(m, jnp.max(s, axis=-1, keepdims=True))
        a = jnp.exp2(m - m_new)
        p = jnp.exp2(s - m_new)
        ms.append(m_new)
        ls.append(a * l + jnp.sum(p, axis=-1, keepdims=True))
        scales.append(a)
        ps.append(p.astype(BF16))
    return tuple((ms[i], ls[i], scales[i] * state[i][2] + jnp.dot(ps[i], vs[i], preferred_element_type=F32))
                 for i in range(n))


def _online_softmax_block(q, k, v, bias, m, l, acc):
    return _online_softmax_blocks([q], [k], [v], bias, ((m, l, acc),))[0]


def _dsa_mask_prompt_kernel(iq_ref, ik_ref, iw_ref, bias_ref, keys_ref, tie_ref, iwb_ref, *, bq, topk):
    qi = pl.program_id(1)
    n_blocks = qi + 1
    q_pos = qi * bq + lax.broadcasted_iota(I32, (bq, 1), 0)
    iw = iw_ref[...]
    for hh in range(N_IDX_HEADS):
        iwb_ref[hh] = jnp.broadcast_to(iw[:, hh:hh + 1], (bq, LANES))

    def score_body(kb, carry):
        s0 = pl.multiple_of(kb * bq, bq)
        ik = ik_ref[pl.ds(s0, bq), :]
        sc = jnp.zeros((bq, bq), F32)
        for hh in range(N_IDX_HEADS):
            d = lax.dot_general(iq_ref[:, hh * IDX_DIM:(hh + 1) * IDX_DIM], ik, (((1,), (1,)), ((), ())),
                                preferred_element_type=F32)
            sc = sc + jnp.maximum(d, 0.0) * jnp.concatenate([iwb_ref[hh]] * (bq // LANES), axis=-1)
        pos = s0 + lax.broadcasted_iota(I32, (bq, bq), 1)
        sc = jnp.where(pos <= q_pos, sc, -jnp.inf)
        keys_ref[:, pl.ds(s0, bq)] = _sortable_key(sc)
        return carry

    lax.fori_loop(0, n_blocks, score_body, 0)
    _topk_bias(keys_ref, bias_ref, tie_ref, n_blocks, bq, q_pos, topk)

    def mask_rest(kb, carry):
        bias_ref[:, pl.ds(pl.multiple_of(kb * bq, bq), bq)] = jnp.full((bq, bq), NEG_BIG, F32)
        return carry

    lax.fori_loop(n_blocks, pl.num_programs(1), mask_rest, 0)


def dsa_mask_prompt(iq, ik, iw, batch, t):
    bq = _pick(t, (256, 128))
    nq = t // bq
    topk = min(TOPK_MAX, t // 4)
    n_iq = N_IDX_HEADS * IDX_DIM
    return pl.pallas_call(
        functools.partial(_dsa_mask_prompt_kernel, bq=bq, topk=topk),
        grid=(batch, nq),
        in_specs=[pl.BlockSpec((bq, n_iq), lambda b, i: (b * nq + i, 0)),
                  pl.BlockSpec((t, IDX_DIM), lambda b, i: (b, 0)),
                  pl.BlockSpec((bq, N_IDX_HEADS), lambda b, i: (b * nq + i, 0))],
        out_specs=pl.BlockSpec((bq, t), lambda b, i: (b * nq + i, 0)),
        out_shape=jax.ShapeDtypeStruct((batch * t, t), F32),
        scratch_shapes=[pltpu.VMEM((bq, t), I32), pltpu.VMEM((bq, 1), I32),
                        pltpu.VMEM((N_IDX_HEADS, bq, LANES), F32)],
        compiler_params=_cparams(("parallel", "arbitrary")),
        name="dsa_mask_prompt",
    )(iq, ik, iw)


def _dsa_prompt_kernel(q_ref, k_ref, v_ref, bias_ref, o_ref, *, bq, hg):
    n_blocks = pl.program_id(2) + 1
    lanes = [slice(h * HEAD_DIM, (h + 1) * HEAD_DIM) for h in range(hg)]
    qs = [q_ref[:, ln] for ln in lanes]

    def body(kb, state):
        s0 = pl.multiple_of(kb * bq, bq)
        return _online_softmax_blocks(qs, [k_ref[pl.ds(s0, bq), ln] for ln in lanes],
                                      [v_ref[pl.ds(s0, bq), ln] for ln in lanes], bias_ref[:, pl.ds(s0, bq)], state)

    state = tuple((jnp.full((bq, 1), NEG_BIG, F32), jnp.zeros((bq, 1), F32), jnp.zeros((bq, HEAD_DIM), F32))
                  for _ in range(hg))
    state = lax.fori_loop(0, n_blocks, body, state)
    for h in range(hg):
        o_ref[:, lanes[h]] = (state[h][2] / state[h][1]).astype(o_ref.dtype)


def dsa_attention_prompt(q, k, v, bias, batch, t):
    bq = _pick(t, (256, 128))
    nq = t // bq
    hg = HEADS_PER_STEP
    return pl.pallas_call(
        functools.partial(_dsa_prompt_kernel, bq=bq, hg=hg),
        grid=(batch, N_HEADS // hg, nq),
        in_specs=[pl.BlockSpec((bq, hg * HEAD_DIM), lambda b, h, i: (b * nq + i, h)),
                  pl.BlockSpec((t, hg * HEAD_DIM), lambda b, h, i: (b, h)),
                  pl.BlockSpec((t, hg * HEAD_DIM), lambda b, h, i: (b, h)),
                  pl.BlockSpec((bq, t), lambda b, h, i: (b * nq + i, 0))],
        out_specs=pl.BlockSpec((bq, hg * HEAD_DIM), lambda b, h, i: (b * nq + i, h)),
        out_shape=jax.ShapeDtypeStruct((batch * t, D_MODEL), BF16),
        compiler_params=_cparams(("parallel", "parallel", "arbitrary")),
        name="dsa_attention_prompt",
    )(q, k, v, bias)


def _idx_score_sample_kernel(pt_ref, iq_ref, iw_ref, *refs, t, pages):
    ik_refs, ikn_ref, o_ref = refs[:pages], refs[pages], refs[pages + 1]
    j = pl.program_id(1)

    def scores(ik):
        d = lax.dot_general(iq_ref[...], ik.astype(BF16), (((1,), (1,)), ((), ())), preferred_element_type=F32)
        wd = jnp.maximum(d, 0.0) * iw_ref[...]
        return jnp.sum(wd.reshape(t, N_IDX_HEADS, PAGE_SIZE), axis=1)

    @pl.when(j < pl.num_programs(1) - 1)
    def _():
        for r in range(pages):
            o_ref[:, r * PAGE_SIZE:(r + 1) * PAGE_SIZE] = scores(ik_refs[r][...])

    @pl.when(j == pl.num_programs(1) - 1)
    def _():
        o_ref[:, 0:PAGE_SIZE] = scores(ikn_ref[...])
        if pages > 1:
            o_ref[:, PAGE_SIZE:] = jnp.zeros((t, (pages - 1) * PAGE_SIZE), F32)


def idx_scores_sample(iq, iw, ik_new, cache_kidx, page_table, db, t):
    n_pages = page_table.shape[1]
    pages = _pick(n_pages, (IDX_PAGES_PER_STEP, 8, 4, 2, 1))
    n_steps = n_pages // pages
    rows = t * N_IDX_HEADS
    iq_rows = iq.reshape(db, rows, IDX_DIM)
    iw_rows = iw.reshape(db, rows, 1)
    ikn = jnp.pad(ik_new.reshape(db, t, IDX_DIM), ((0, 0), (0, PAGE_SIZE - t), (0, 0)))
    page_specs = [pl.BlockSpec((None, PAGE_SIZE, IDX_DIM),
                               lambda b, j, pt, r=r: (pt[b, jnp.minimum(j, n_steps - 1) * pages + r], 0, 0))
                  for r in range(pages)]
    grid_spec = pltpu.PrefetchScalarGridSpec(
        num_scalar_prefetch=1,
        grid=(db, n_steps + 1),
        in_specs=[pl.BlockSpec((None, rows, IDX_DIM), lambda b, j, pt: (b, 0, 0)),
                  pl.BlockSpec((None, rows, 1), lambda b, j, pt: (b, 0, 0))]
        + page_specs
        + [pl.BlockSpec((None, PAGE_SIZE, IDX_DIM), lambda b, j, pt: (b, 0, 0))],
        out_specs=pl.BlockSpec((None, t, pages * PAGE_SIZE), lambda b, j, pt: (b, 0, j)),
    )
    return pl.pallas_call(
        functools.partial(_idx_score_sample_kernel, t=t, pages=pages),
        grid_spec=grid_spec,
        out_shape=jax.ShapeDtypeStruct((db, t, (n_pages + pages) * PAGE_SIZE), F32),
        compiler_params=_cparams(("parallel", "arbitrary")),
        name="idx_scores_sample",
    )(page_table, iq_rows, iw_rows, *([cache_kidx] * pages), ikn)


def _topk_sample_kernel(s_ref, qpos_ref, bias_ref, keys_ref, tie_ref, *, topk, n_blocks):
    q_pos = qpos_ref[...]
    rows = q_pos.shape[0]

    def fill(kb, carry):
        s0 = pl.multiple_of(kb * LANES, LANES)
        pos = s0 + lax.broadcasted_iota(I32, (rows, LANES), 1)
        sc = jnp.where(pos <= q_pos, s_ref[:, pl.ds(s0, LANES)], -jnp.inf)
        keys_ref[:, pl.ds(s0, LANES)] = _sortable_key(sc)
        return carry

    lax.fori_loop(0, n_blocks, fill, 0)
    _topk_bias(keys_ref, bias_ref, tie_ref, n_blocks, LANES, q_pos, topk)


def topk_bias_sample(scores, q_pos, topk):
    r, l = scores.shape
    return pl.pallas_call(
        functools.partial(_topk_sample_kernel, topk=topk, n_blocks=l // LANES),
        out_shape=jax.ShapeDtypeStruct((r, l), F32),
        scratch_shapes=[pltpu.VMEM((r, l), I32), pltpu.VMEM((r, 1), I32)],
        compiler_params=pltpu.CompilerParams(vmem_limit_bytes=VMEM_LIMIT_BYTES),
        name="topk_bias_sample",
    )(scores, q_pos)


def _dsa_sample_kernel(pt_ref, q_ref, bias_ref, *refs, t, pages):
    n_cache = 2 * N_HEAD_GROUPS * pages
    cache_refs = refs[:n_cache]
    kn_ref, vn_ref, o_ref, m_ref, l_ref, acc_ref = refs[n_cache:]
    j = pl.program_id(1)
    rows = t * N_HEADS
    q = q_ref[...]

    @pl.when(j == 0)
    def _():
        m_ref[...] = jnp.full((rows, 1), NEG_BIG, F32)
        l_ref[...] = jnp.zeros((rows, 1), F32)
        acc_ref[...] = jnp.zeros((rows, D_MODEL), F32)

    def head_rows(bias):
        return jnp.broadcast_to(bias[:, None, :], (t, N_HEADS, bias.shape[1])).reshape(rows, bias.shape[1])

    @pl.when(j < pl.num_programs(1) - 1)
    def _():
        k = jnp.concatenate([_load_page(_page_refs(cache_refs, r)[0]) for r in range(pages)], axis=0)
        v = jnp.concatenate([_load_page(_page_refs(cache_refs, r)[1]) for r in range(pages)], axis=0)
        m_ref[...], l_ref[...], acc_ref[...] = _online_softmax_block(
            q, k, v, head_rows(bias_ref[...]), m_ref[...], l_ref[...], acc_ref[...])

    @pl.when(j == pl.num_programs(1) - 1)
    def _():
        m, l, acc = _online_softmax_block(q, kn_ref[...].astype(BF16), vn_ref[...].astype(BF16),
                                          head_rows(bias_ref[:, 0:PAGE_SIZE]), m_ref[...], l_ref[...], acc_ref[...])
        o_ref[...] = _extract_block_diag(acc / l, t)


def dsa_attention_sample(q, bias, k_new, v_new, cache_k, cache_v, page_table, db, t):
    n_pages = page_table.shape[1]
    pages = _pick(n_pages, (SAMPLE_PAGES_PER_STEP, 2, 1))
    n_steps = n_pages // pages
    assert bias.shape[2] >= (n_steps + 1) * pages * PAGE_SIZE
    rows = t * N_HEADS
    qbd = _block_diag_queries(q, db, t)

    def page_of(b, j, pt, r):
        return pt[b, jnp.minimum(j, n_steps - 1) * pages + r]

    grid_spec = pltpu.PrefetchScalarGridSpec(
        num_scalar_prefetch=1,
        grid=(db, n_steps + 1),
        in_specs=[pl.BlockSpec((None, rows, D_MODEL), lambda b, j, pt: (b, 0, 0)),
                  pl.BlockSpec((None, t, pages * PAGE_SIZE), lambda b, j, pt: (b, 0, j))]
        + _cache_specs(pages, page_of)
        + [pl.BlockSpec((None, PAGE_SIZE, D_MODEL), lambda b, j, pt: (b, 0, 0)),
           pl.BlockSpec((None, PAGE_SIZE, D_MODEL), lambda b, j, pt: (b, 0, 0))],
        out_specs=pl.BlockSpec((None, t, D_MODEL), lambda b, j, pt: (b, 0, 0)),
        scratch_shapes=[pltpu.VMEM((rows, 1), F32), pltpu.VMEM((rows, 1), F32), pltpu.VMEM((rows, D_MODEL), F32)],
    )
    out = pl.pallas_call(
        functools.partial(_dsa_sample_kernel, t=t, pages=pages),
        grid_spec=grid_spec,
        out_shape=jax.ShapeDtypeStruct((db, t, D_MODEL), F32),
        compiler_params=_cparams(("parallel", "arbitrary")),
        name="dsa_attention_sample",
    )(page_table, qbd, bias, *_cache_args(cache_k, cache_v, pages), _pad_new_page(k_new, db, t),
      _pad_new_page(v_new, db, t))
    return out.reshape(db * t, D_MODEL)


def _ffn(x, f_norm, wg, wu, wd):
    h = rmsnorm_rows(x, f_norm)
    return matmul_residual(ffn_up(h, wg, wu), wd, x)


def _sb_projections(x, attn_norm, w_qkv, q_norm, k_norm):
    h = rmsnorm_rows(x, attn_norm)
    (q,) = project(h, w_qkv, 0, D_MODEL, (BF16,), gain=q_norm)
    k32, k16 = project(h, w_qkv, D_MODEL, D_MODEL, (F32, BF16), gain=k_norm)
    v32, v16 = project(h, w_qkv, 2 * D_MODEL, D_MODEL, (F32, BF16))
    return q, k32, k16, v32, v16


def _dsa_projections(x, attn_norm, w_in, w_small, q_norm, k_norm, rope_qk, rope_idx):
    h = rmsnorm_rows(x, attn_norm)
    (q,) = project(h, w_in, 0, D_MODEL, (BF16,), gain=q_norm, rope=rope_qk, rope_shift=ROT_DIM // 2)
    k32, k16 = project(h, w_in, D_MODEL, D_MODEL, (F32, BF16), gain=k_norm, rope=rope_qk, rope_shift=ROT_DIM // 2)
    v32, v16 = project(h, w_in, 2 * D_MODEL, D_MODEL, (F32, BF16))
    (iq,) = project(h, w_in, 3 * D_MODEL, N_IDX_HEADS * IDX_DIM, (BF16,), rope=rope_idx, rope_shift=IDX_ROT_DIM // 2)
    (tail_rot,) = project(h, w_small, 0, LANES, (F32,), rope=rope_idx, rope_shift=IDX_ROT_DIM // 2)
    (tail_raw,) = project(h, w_small, 0, LANES, (F32,))
    ik32 = tail_rot[:, :IDX_DIM]
    iw = tail_raw[:, IDX_DIM:IDX_DIM + N_IDX_HEADS]
    return q, k32, k16, v32, v16, iq, ik32, iw


def kernel(x_prompt, x_sample, cache_k_0, cache_v_0, state_pool_1, cache_k_2, cache_v_2, cache_kidx_2, cache_k_3, cache_v_3, page_table, attn_norm_0, w_qkv_0, q_norm_0, k_norm_0, w_o_0, ffn_norm_0, w_gate_0, w_up_0, w_down_0, attn_norm_1, w_pool_1, pool_scale_1, ffn_norm_1, w_gate_1, w_up_1, w_down_1, attn_norm_2, w_in_2, q_norm_2, k_norm_2, w_o_2, ffn_norm_2, w_gate_2, w_up_2, w_down_2, attn_norm_3, w_qkv_3, q_norm_3, k_norm_3, w_o_3, ffn_norm_3, w_gate_3, w_up_3, w_down_3):
    batch, t, d = x_prompt.shape
    db, dt, _ = x_sample.shape
    n_phys = cache_k_0.shape[0]
    n_pages = page_table.shape[1]
    past = n_pages * PAGE_SIZE
    mp, ms = batch * t, db * dt
    xp = x_prompt.reshape(mp, d)
    xs = x_sample.reshape(ms, d)

    def bf(w):
        return w.astype(BF16)

    def heads(a, b_, t_):
        return a.reshape(b_, t_, N_HEADS, HEAD_DIM)

    def sb_layer(xp, xs, cache_k, cache_v, attn_norm, w_qkv, q_norm, k_norm, w_o):
        q, k32, k16, v32, v16 = _sb_projections(xp, attn_norm, w_qkv, q_norm, k_norm)
        xp = matmul_residual(sb_attention_prompt(q, k16, v16, batch, t), w_o, xp)
        qs, ks32, _, vs32, _ = _sb_projections(xs, attn_norm, w_qkv, q_norm, k_norm)
        o_s = sb_attention_sample(qs, ks32, vs32, cache_k, cache_v, page_table, db, dt)
        xs = matmul_residual(o_s.astype(BF16), w_o, xs)
        return xp, xs, (heads(k32, batch, t), heads(v32, batch, t)), (heads(ks32, db, dt), heads(vs32, db, dt))

    xp, xs, (k0_p, v0_p), (k0_s, v0_s) = sb_layer(xp, xs, cache_k_0, cache_v_0, attn_norm_0, w_qkv_0, q_norm_0,
                                                   k_norm_0, w_o_0)
    xp = _ffn(xp, ffn_norm_0, w_gate_0, w_up_0, w_down_0)
    xs = _ffn(xs, ffn_norm_0, w_gate_0, w_up_0, w_down_0)

    w_pool = bf(w_pool_1)
    pool1_p = rmsnorm_rows(xp.reshape(batch, t, d)[:, t - POOL_HALO:].reshape(batch * POOL_HALO, d), attn_norm_1,
                           F32).reshape(batch, POOL_HALO, d)[:, POOL_HALO - POOL_STATE:]
    hs_new = rmsnorm_rows(xs, attn_norm_1, F32)
    pool1_s = jnp.concatenate([state_pool_1, hs_new.reshape(db, dt, d)], axis=1)[:, -POOL_STATE:]
    xp = pool_layer(xp, xp, attn_norm_1, w_pool, pool_scale_1, seq_len=t, halo_is_state=False, pos0=0)
    rows_s = 8
    xs_pad = jnp.pad(xs.reshape(db, dt, d), ((0, 0), (0, rows_s - dt), (0, 0))).reshape(db * rows_s, d)
    state_pad = jnp.pad(state_pool_1, ((0, 0), (POOL_HALO - POOL_STATE, 0), (0, 0))).reshape(db * POOL_HALO, d)
    xs = pool_layer(xs_pad, state_pad, attn_norm_1, w_pool, pool_scale_1, seq_len=rows_s, halo_is_state=True,
                    pos0=PAST_LEN).reshape(db, rows_s, d)[:, :dt].reshape(ms, d)
    xp = _ffn(xp, ffn_norm_1, w_gate_1, w_up_1, w_down_1)
    xs = _ffn(xs, ffn_norm_1, w_gate_1, w_up_1, w_down_1)

    w_in, w_o2 = w_in_2, w_o_2
    n_main = 3 * D_MODEL + N_IDX_HEADS * IDX_DIM
    w_small = jnp.pad(w_in[:, n_main:], ((0, 0), (0, LANES - (w_in.shape[1] - n_main))))
    pos_p = jnp.arange(t, dtype=I32)
    pos_s = jnp.tile(past + jnp.arange(dt, dtype=I32), db)
    q, k32, k16, v32, v16, iq, ik32, iw = _dsa_projections(
        xp, attn_norm_2, w_in, w_small, q_norm_2, k_norm_2,
        rope_tables(pos_p, ROT_DIM, HEAD_DIM), rope_tables(pos_p, IDX_ROT_DIM, IDX_DIM))
    iw_scaled = iw * (IDX_W_SCALE * IDX_SCALE)
    o_p = dsa_attention_prompt(q, k16, v16, dsa_mask_prompt(iq, ik32.astype(BF16), iw_scaled, batch, t), batch, t)
    k2_p, v2_p, kidx2_p = heads(k32, batch, t), heads(v32, batch, t), ik32.reshape(batch, t, IDX_DIM)
    xp = matmul_residual(o_p, w_o2, xp)

    qs, ks32, _, vs32, _, iqs, iks32, iws = _dsa_projections(
        xs, attn_norm_2, w_in, w_small, q_norm_2, k_norm_2,
        rope_tables(pos_s, ROT_DIM, HEAD_DIM), rope_tables(pos_s, IDX_ROT_DIM, IDX_DIM))
    scores = idx_scores_sample(iqs, iws * (IDX_W_SCALE * IDX_SCALE), iks32, cache_kidx_2, page_table, db, dt)
    topk_s = min(TOPK_MAX, (past + dt) // 4)
    bias_s = topk_bias_sample(scores.reshape(ms, -1), pos_s.reshape(ms, 1), topk_s).reshape(db, dt, -1)
    o_s = dsa_attention_sample(qs, bias_s, ks32, vs32, cache_k_2, cache_v_2, page_table, db, dt)
    k2_s, v2_s, kidx2_s = heads(ks32, db, dt), heads(vs32, db, dt), iks32.reshape(db, dt, IDX_DIM)
    xs = matmul_residual(o_s.astype(BF16), w_o2, xs)
    xp = _ffn(xp, ffn_norm_2, w_gate_2, w_up_2, w_down_2)
    xs = _ffn(xs, ffn_norm_2, w_gate_2, w_up_2, w_down_2)

    xp, xs, (k3_p, v3_p), (k3_s, v3_s) = sb_layer(xp, xs, cache_k_3, cache_v_3, attn_norm_3, w_qkv_3, q_norm_3,
                                                   k_norm_3, w_o_3)
    xp = _ffn(xp, ffn_norm_3, w_gate_3, w_up_3, w_down_3)
    xs = _ffn(xs, ffn_norm_3, w_gate_3, w_up_3, w_down_3)

    return (xp.reshape(batch, t, d), xs.reshape(db, dt, d), k0_p, v0_p, k0_s, v0_s, pool1_p, pool1_s,
            k2_p, v2_p, kidx2_p, k2_s, v2_s, kidx2_s, k3_p, v3_p, k3_s, v3_s)
```

```python
import functools
import math

import jax
import jax.numpy as jnp
from jax import lax
from jax.experimental import pallas as pl
from jax.experimental.pallas import tpu as pltpu

F32 = jnp.float32
BF16 = jnp.bfloat16
I32 = jnp.int32

D_MODEL = 2048
N_HEADS = 16
HEAD_DIM = 128
ROT_DIM = 32
ROPE_THETA = 500000.0
POOL_WINDOWS = (2, 4, 8, 16)
POOL_GROUP_DIM = 512
POOL_STATE = 15
N_IDX_HEADS = 16
IDX_DIM = 64
IDX_ROT_DIM = 16
TOPK_MAX = 256
PAGE_SIZE = 128
PAST_LEN = 16384
RMS_EPS = 1e-6
ATTN_SCALE = HEAD_DIM ** -0.5
IDX_SCALE = IDX_DIM ** -0.5
IDX_W_SCALE = N_IDX_HEADS ** -0.5
LOG2E = 1.4426950408889634

HEADS_PER_STEP = 4
SAMPLE_PAGES_PER_STEP = 4
IDX_PAGES_PER_STEP = 16
HEAD_GROUP = 8
PROJ_ROW_CHUNK = 256
DSA_KEY_BLOCK = 512
DSA_HEADS_PER_STEP = 4

LANES = 128
POOL_HALO = 16
VMEM_LIMIT_BYTES = 56 * 1024 * 1024
NEG_BIG = -1e30
INT_MIN = -2 ** 31
SB_DEAD_LOG2 = -160.0


def _cparams(sem):
    return pltpu.CompilerParams(dimension_semantics=sem, vmem_limit_bytes=VMEM_LIMIT_BYTES)


def _pick(n, cands):
    for c in cands:
        if n % c == 0:
            return c
    return n


def _rmsnorm_kernel(x_ref, g_ref, o_ref):
    x = x_ref[...]
    ms = jnp.mean(x * x, axis=-1, keepdims=True)
    o_ref[...] = (x * lax.rsqrt(ms + RMS_EPS) * g_ref[...]).astype(o_ref.dtype)


def rmsnorm_rows(x, g, out_dtype=BF16):
    m, d = x.shape
    bm = _pick(m, (512, 256, 128, 64, 32, 16, 8))
    return pl.pallas_call(
        _rmsnorm_kernel,
        grid=(m // bm,),
        in_specs=[pl.BlockSpec((bm, d), lambda i: (i, 0)),
                  pl.BlockSpec((1, d), lambda i: (0, 0))],
        out_specs=pl.BlockSpec((bm, d), lambda i: (i, 0)),
        out_shape=jax.ShapeDtypeStruct((m, d), out_dtype),
        compiler_params=_cparams(("parallel",)),
        name="rmsnorm",
    )(x, g.reshape(1, d))


def _proj_kernel(*refs, norm, rope_shift, scale, n_out, row_chunk):
    x_ref, w_ref = refs[0], refs[1]
    pos = 2
    g_ref = None
    if norm:
        g_ref = refs[pos]
        pos += 1
    tabs = None
    if rope_shift:
        tabs = refs[pos:pos + 3]
        pos += 3
    out_refs = refs[pos:pos + n_out]
    w_bf = refs[pos + n_out]
    bm, bn = out_refs[0].shape

    @pl.when(pl.program_id(1) == 0)
    def _():
        w_bf[...] = w_ref[...].astype(BF16)

    w = w_bf[...]
    chunks = [slice(r, r + row_chunk) for r in range(0, bm, row_chunk)]
    ys = [jnp.dot(x_ref[rows, :], w, preferred_element_type=F32) for rows in chunks]
    for rows, y in zip(chunks, ys):
        parts = []
        for h in range(bn // LANES):
            yh = y[:, h * LANES:(h + 1) * LANES]
            if norm:
                ms = jnp.mean(yh * yh, axis=-1, keepdims=True)
                yh = yh * lax.rsqrt(ms + RMS_EPS) * g_ref[:, h * LANES:(h + 1) * LANES]
            if rope_shift:
                c, sa, sb = (tab[rows, :] for tab in tabs)
                yh = yh * c + pltpu.roll(yh, rope_shift, 1) * sa + pltpu.roll(yh, LANES - rope_shift, 1) * sb
            if scale is not None:
                yh = yh * scale
            parts.append(yh)
        y = jnp.concatenate(parts, axis=-1)
        for o_ref in out_refs:
            o_ref[rows, :] = y.astype(o_ref.dtype)


def project(h, w, col0, n, out_dtypes, *, gain=None, rope=None, rope_shift=0, scale=None, bn=512):
    m, k = h.shape
    bm = _pick(m, (1024, 512, 256, 128, 64, 32, 16))
    bn = min(bn, n)
    assert n % bn == 0 and col0 % bn == 0 and bn % LANES == 0
    off = col0 // bn
    in_specs = [pl.BlockSpec((bm, k), lambda j, i: (i, 0)),
                pl.BlockSpec((k, bn), lambda j, i: (0, j + off))]
    args = [h, w]
    if gain is not None:
        in_specs.append(pl.BlockSpec((1, bn), lambda j, i: (0, 0)))
        args.append(jnp.tile(gain.reshape(1, -1), (1, bn // gain.shape[-1])))
    if rope is not None:
        period = rope[0].shape[0] // bm
        for tab in rope:
            in_specs.append(pl.BlockSpec((bm, LANES), lambda j, i: (i % period, 0)))
            args.append(tab)
    outs = pl.pallas_call(
        functools.partial(_proj_kernel, norm=gain is not None, rope_shift=rope_shift if rope is not None else 0,
                          scale=scale, n_out=len(out_dtypes), row_chunk=_pick(bm, (PROJ_ROW_CHUNK,))),
        grid=(n // bn, m // bm),
        in_specs=in_specs,
        out_specs=[pl.BlockSpec((bm, bn), lambda j, i: (i, j)) for _ in out_dtypes],
        out_shape=[jax.ShapeDtypeStruct((m, n), dt) for dt in out_dtypes],
        scratch_shapes=[pltpu.VMEM((k, bn), BF16)],
        compiler_params=_cparams(("arbitrary", "arbitrary")),
        name="project",
    )(*args)
    return outs


def rope_tables(pos, rot_dim, head_dim):
    half = rot_dim // 2
    inv_freq = ROPE_THETA ** (-jnp.arange(half, dtype=F32) / half)
    ang = pos.astype(F32)[:, None] * inv_freq[None, :]
    cos, sin = jnp.cos(ang), jnp.sin(ang)
    p = pos.shape[0]
    ones = jnp.ones((p, head_dim - rot_dim), F32)
    zeros_h = jnp.zeros((p, half), F32)
    zeros_r = jnp.zeros((p, head_dim - rot_dim), F32)
    c = jnp.concatenate([cos, cos, ones], axis=-1)
    sa = jnp.concatenate([zeros_h, sin, zeros_r], axis=-1)
    sb = jnp.concatenate([-sin, zeros_h, zeros_r], axis=-1)
    reps = LANES // head_dim
    return tuple(jnp.tile(t, (1, reps)) for t in (c, sa, sb))


def _mm_res_kernel(a_ref, w_ref, r_ref, o_ref, w_bf):
    @pl.when(pl.program_id(1) == 0)
    def _():
        w_bf[...] = w_ref[...].astype(BF16)

    o_ref[...] = r_ref[...] + jnp.dot(a_ref[...], w_bf[...], preferred_element_type=F32)


def matmul_residual(a, w, res):
    m, k = a.shape
    n = w.shape[1]
    bm = _pick(m, (1024 if k <= D_MODEL else 512, 512, 256, 128, 64, 32, 16))
    bn = _pick(n, (512, 256, 128))
    return pl.pallas_call(
        _mm_res_kernel,
        grid=(n // bn, m // bm),
        in_specs=[pl.BlockSpec((bm, k), lambda j, i: (i, 0)),
                  pl.BlockSpec((k, bn), lambda j, i: (0, j)),
                  pl.BlockSpec((bm, bn), lambda j, i: (i, j))],
        out_specs=pl.BlockSpec((bm, bn), lambda j, i: (i, j)),
        out_shape=jax.ShapeDtypeStruct((m, n), F32),
        scratch_shapes=[pltpu.VMEM((k, bn), BF16)],
        compiler_params=_cparams(("arbitrary", "arbitrary")),
        name="matmul_residual",
    )(a, w, res)


def _ffn_up_kernel(x_ref, wg_ref, wu_ref, o_ref, wg_bf, wu_bf):
    @pl.when(pl.program_id(1) == 0)
    def _():
        wg_bf[...] = wg_ref[...].astype(BF16)
        wu_bf[...] = wu_ref[...].astype(BF16)

    x = x_ref[...]
    g = jnp.dot(x, wg_bf[...], preferred_element_type=F32)
    u = jnp.dot(x, wu_bf[...], preferred_element_type=F32)
    o_ref[...] = (g * (1.0 / (1.0 + jnp.exp(-g))) * u).astype(o_ref.dtype)


def ffn_up(h, wg, wu):
    m, k = h.shape
    n = wg.shape[1]
    bm = _pick(m, (1024, 512, 256, 128, 64, 32, 16))
    bn = _pick(n, (512, 256, 128))
    return pl.pallas_call(
        _ffn_up_kernel,
        grid=(n // bn, m // bm),
        in_specs=[pl.BlockSpec((bm, k), lambda j, i: (i, 0)),
                  pl.BlockSpec((k, bn), lambda j, i: (0, j)),
                  pl.BlockSpec((k, bn), lambda j, i: (0, j))],
        out_specs=pl.BlockSpec((bm, bn), lambda j, i: (i, j)),
        out_shape=jax.ShapeDtypeStruct((m, n), BF16),
        scratch_shapes=[pltpu.VMEM((k, bn), BF16), pltpu.VMEM((k, bn), BF16)],
        compiler_params=_cparams(("arbitrary", "arbitrary")),
        name="ffn_up",
    )(h, wg, wu)


def _sb_blocks(qs, ks, vs, u2, state, vis):
    n = len(qs)
    zs = [lax.dot_general(qs[i], ks[i], (((1,), (1,)), ((), ())), preferred_element_type=F32) for i in range(n)]
    log_betas, sums, hls = [], [], []
    for z in zs:
        z = z * (ATTN_SCALE * LOG2E)
        neg_abs = pltpu.bitcast(pltpu.bitcast(z, I32) | jnp.int32(INT_MIN), F32)
        log_beta = jnp.minimum(z, 0.0) - jnp.log(1.0 + jnp.exp2(neg_abs)) * LOG2E
        log_keep = log_beta - z
        if vis is not None:
            log_keep = jnp.where(vis, log_keep, 0.0)
        hi = log_keep.astype(BF16)
        lo = (log_keep - hi.astype(F32)).astype(BF16)
        log_betas.append(log_beta)
        sums.append(jnp.sum(log_keep, axis=-1, keepdims=True))
        hls.append(jnp.concatenate([hi, lo], axis=-1))
    tails = [jnp.dot(hl, u2, preferred_element_type=F32) for hl in hls]
    ws = []
    for i in range(n):
        w = jnp.exp2(log_betas[i] + (tails[i] + state[i][0]))
        if vis is not None:
            w = jnp.where(vis, w, 0.0)
        ws.append(w.astype(BF16))
    return tuple((state[i][0] + sums[i], state[i][1] + jnp.dot(ws[i], vs[i], preferred_element_type=F32))
                 for i in range(n))


def _sb_block(q, k, v, u2, carry, acc, vis):
    return _sb_blocks([q], [k], [v], u2, ((carry, acc),), vis)[0]


def _sb_any_live(state):
    live = state[0][0]
    for carry, _ in state[1:]:
        live = jnp.maximum(live, carry)
    return (jnp.max(live) >= SB_DEAD_LOG2).astype(I32)


def _suffix_matrix2(c):
    row = lax.broadcasted_iota(I32, (2 * c, c), 0)
    col = lax.broadcasted_iota(I32, (2 * c, c), 1)
    return (jnp.where(row < c, row, row - c) > col).astype(BF16)


def _sb_prompt_kernel(q_ref, k_ref, v_ref, o_ref, *, bq, hg):
    qi = pl.program_id(2)
    u2 = _suffix_matrix2(bq)
    row = lax.broadcasted_iota(I32, (bq, bq), 0)
    col = lax.broadcasted_iota(I32, (bq, bq), 1)
    lanes = [slice(h * HEAD_DIM, (h + 1) * HEAD_DIM) for h in range(hg)]
    qs = [q_ref[:, ln] for ln in lanes]

    def blocks(s0, state, vis):
        return _sb_blocks(qs, [k_ref[pl.ds(s0, bq), ln] for ln in lanes], [v_ref[pl.ds(s0, bq), ln] for ln in lanes],
                          u2, state, vis)

    state = tuple((jnp.zeros((bq, 1), F32), jnp.zeros((bq, HEAD_DIM), F32)) for _ in range(hg))
    state = blocks(pl.multiple_of(qi * bq, bq), state, col < row)

    def step(c):
        i, _, st = c
        st = blocks(pl.multiple_of((qi - 1 - i) * bq, bq), st, None)
        return i + 1, _sb_any_live(st), st

    _, _, state = lax.while_loop(lambda c: (c[0] < qi) & (c[1] > 0), step, (jnp.int32(0), _sb_any_live(state), state))
    for h in range(hg):
        o_ref[:, lanes[h]] = state[h][1].astype(o_ref.dtype)


def sb_attention_prompt(q, k, v, batch, t):
    bq = _pick(t, (256, 128))
    nq = t // bq
    hg = HEADS_PER_STEP
    return pl.pallas_call(
        functools.partial(_sb_prompt_kernel, bq=bq, hg=hg),
        grid=(batch, N_HEADS // hg, nq),
        in_specs=[pl.BlockSpec((bq, hg * HEAD_DIM), lambda b, h, i: (b * nq + i, h)),
                  pl.BlockSpec((t, hg * HEAD_DIM), lambda b, h, i: (b, h)),
                  pl.BlockSpec((t, hg * HEAD_DIM), lambda b, h, i: (b, h))],
        out_specs=pl.BlockSpec((bq, hg * HEAD_DIM), lambda b, h, i: (b * nq + i, h)),
        out_shape=jax.ShapeDtypeStruct((batch * t, D_MODEL), BF16),
        compiler_params=_cparams(("parallel", "parallel", "arbitrary")),
        name="sb_attention_prompt",
    )(q, k, v)


def _block_diag_queries(q, db, t):
    q5 = q.reshape(db, t, N_HEADS, 1, HEAD_DIM)
    eye = jnp.eye(N_HEADS, dtype=jnp.bool_).reshape(1, 1, N_HEADS, N_HEADS, 1)
    return jnp.where(eye, q5, jnp.zeros_like(q5)).reshape(db, t * N_HEADS, D_MODEL)


def _extract_block_diag(acc, t):
    r = lax.broadcasted_iota(I32, acc.shape, 0)
    c = lax.broadcasted_iota(I32, acc.shape, 1)
    keep = (r % N_HEADS) == (c // HEAD_DIM)
    return jnp.sum(jnp.where(keep, acc, 0.0).reshape(t, N_HEADS, acc.shape[1]), axis=1)


N_HEAD_GROUPS = N_HEADS // HEAD_GROUP


def _load_page(group_refs):
    parts = []
    for ref in group_refs:
        rows = ref.reshape(PAGE_SIZE * HEAD_GROUP, HEAD_DIM)
        parts += [rows[pl.ds(h, PAGE_SIZE, stride=HEAD_GROUP), :] for h in range(HEAD_GROUP)]
    return jnp.concatenate(parts, axis=-1).astype(BF16)


def _page_refs(cache_refs, r):
    base = 2 * N_HEAD_GROUPS * r
    return cache_refs[base:base + N_HEAD_GROUPS], cache_refs[base + N_HEAD_GROUPS:base + 2 * N_HEAD_GROUPS]


def _cache_specs(pages, page_of):
    specs = []
    for r in range(pages):
        for _ in range(2):
            for g in range(N_HEAD_GROUPS):
                specs.append(pl.BlockSpec((None, PAGE_SIZE, HEAD_GROUP, HEAD_DIM),
                                          lambda b, j, pt, r=r, g=g: (page_of(b, j, pt, r), 0, g, 0)))
    return specs


def _cache_args(cache_k, cache_v, pages):
    return ([cache_k] * N_HEAD_GROUPS + [cache_v] * N_HEAD_GROUPS) * pages


def _load_flat_page(page_ref):
    return jnp.concatenate([page_ref[pl.ds(h, PAGE_SIZE, stride=N_HEADS), :] for h in range(N_HEADS)],
                           axis=-1).astype(BF16)


def _sb_sample_kernel(pt_ref, q_ref, kn_ref, vn_ref, ck_hbm, cv_hbm, o_ref, kbuf, vbuf, sem, carry_ref, acc_ref,
                      *, t, n_pages):
    b = pl.program_id(0)
    rows = t * N_HEADS
    q = q_ref[...]

    def page_copies(i, slot):
        page = pt_ref[b, n_pages - 1 - i]
        return (pltpu.make_async_copy(ck_hbm.at[page], kbuf.at[slot], sem.at[0, slot]),
                pltpu.make_async_copy(cv_hbm.at[page], vbuf.at[slot], sem.at[1, slot]))

    def start(i, slot):
        for cp in page_copies(i, slot):
            cp.start()

    def wait(i, slot):
        for cp in page_copies(i, slot):
            cp.wait()

    start(0, 0)
    r = lax.broadcasted_iota(I32, (rows, PAGE_SIZE), 0)
    c = lax.broadcasted_iota(I32, (rows, PAGE_SIZE), 1)
    carry, acc = _sb_block(q, kn_ref[...].astype(BF16), vn_ref[...].astype(BF16), _suffix_matrix2(PAGE_SIZE),
                           jnp.zeros((rows, 1), F32), jnp.zeros((rows, D_MODEL), F32), c < r // N_HEADS)
    carry_ref[...] = carry
    acc_ref[...] = acc
    u2 = _suffix_matrix2(PAGE_SIZE)

    def body(c):
        i, _ = c
        slot = lax.rem(i, 2)
        wait(i, slot)

        @pl.when(i + 1 < n_pages)
        def _():
            start(i + 1, 1 - slot)

        carry, acc = _sb_block(q, _load_flat_page(kbuf.at[slot]), _load_flat_page(vbuf.at[slot]), u2,
                               carry_ref[...], acc_ref[...], None)
        carry_ref[...] = carry
        acc_ref[...] = acc
        return i + 1, _sb_any_live(((carry, acc),))

    i_end, _ = lax.while_loop(lambda c: (c[0] < n_pages) & (c[1] > 0), body,
                              (jnp.int32(0), _sb_any_live(((carry, acc),))))

    @pl.when(i_end < n_pages)
    def _():
        wait(i_end, lax.rem(i_end, 2))

    o_ref[...] = _extract_block_diag(acc_ref[...], t)


def _pad_new_page(x, db, t):
    return jnp.pad(x.reshape(db, t, D_MODEL), ((0, 0), (0, PAGE_SIZE - t), (0, 0)))


def sb_attention_sample(q, k_new, v_new, cache_k, cache_v, page_table, db, t):
    n_phys = cache_k.shape[0]
    n_pages = page_table.shape[1]
    rows = t * N_HEADS
    qbd = _block_diag_queries(q, db, t)
    page_rows = PAGE_SIZE * N_HEADS

    def flat(cache):
        return cache.reshape(n_phys, page_rows, HEAD_DIM)

    grid_spec = pltpu.PrefetchScalarGridSpec(
        num_scalar_prefetch=1,
        grid=(db,),
        in_specs=[pl.BlockSpec((None, rows, D_MODEL), lambda b, pt: (b, 0, 0)),
                  pl.BlockSpec((None, PAGE_SIZE, D_MODEL), lambda b, pt: (b, 0, 0)),
                  pl.BlockSpec((None, PAGE_SIZE, D_MODEL), lambda b, pt: (b, 0, 0)),
                  pl.BlockSpec(memory_space=pl.ANY),
                  pl.BlockSpec(memory_space=pl.ANY)],
        out_specs=pl.BlockSpec((None, t, D_MODEL), lambda b, pt: (b, 0, 0)),
        scratch_shapes=[pltpu.VMEM((2, page_rows, HEAD_DIM), F32), pltpu.VMEM((2, page_rows, HEAD_DIM), F32),
                        pltpu.SemaphoreType.DMA((2, 2)),
                        pltpu.VMEM((rows, 1), F32), pltpu.VMEM((rows, D_MODEL), F32)],
    )
    out = pl.pallas_call(
        functools.partial(_sb_sample_kernel, t=t, n_pages=n_pages),
        grid_spec=grid_spec,
        out_shape=jax.ShapeDtypeStruct((db, t, D_MODEL), F32),
        compiler_params=_cparams(("arbitrary",)),
        name="sb_attention_sample",
    )(page_table, qbd, _pad_new_page(k_new, db, t), _pad_new_page(v_new, db, t), flat(cache_k), flat(cache_v))
    return out.reshape(db * t, D_MODEL)


def _pool_kernel(x_ref, halo_ref, g_ref, w_ref, ps_ref, o_ref, ext_ref, *, bm, blocks_per_seq, halo_is_state, pos0):
    i = pl.program_id(0)
    g = g_ref[...]

    def norm(x):
        ms = jnp.mean(x * x, axis=-1, keepdims=True)
        return x * lax.rsqrt(ms + RMS_EPS) * g

    x = x_ref[...]
    h = norm(x)
    if halo_is_state:
        halo = halo_ref[...]
    else:
        halo = jnp.where(i % blocks_per_seq == 0, 0.0, norm(halo_ref[...]))
    ext_ref[0:POOL_HALO, :] = halo
    ext_ref[POOL_HALO:POOL_HALO + bm, :] = h
    t_in_seq = (i % blocks_per_seq) * bm + lax.broadcasted_iota(I32, (bm, 1), 0) + pos0
    for gi, win in enumerate(POOL_WINDOWS):
        c0 = gi * POOL_GROUP_DIM
        c1 = c0 + POOL_GROUP_DIM
        wsum = h[:, c0:c1]
        for d in range(1, win):
            wsum = wsum + ext_ref[POOL_HALO - d:POOL_HALO - d + bm, c0:c1]
        count = jnp.minimum(t_in_seq + 1, win).astype(F32)
        pooled = wsum / count - h[:, c0:c1]
        y = jnp.dot(pooled.astype(BF16), w_ref[gi], preferred_element_type=F32)
        o_ref[:, c0:c1] = x[:, c0:c1] + y * ps_ref[:, c0:c1]


def pool_layer(x, halo_src, gain, w_pool, pool_scale, *, seq_len, halo_is_state, pos0):
    m, d = x.shape
    bm = _pick(seq_len, (256, 128, 64, 32, 16, 8))
    blocks_per_seq = seq_len // bm
    if halo_is_state:
        halo_spec = pl.BlockSpec((POOL_HALO, d), lambda i: (i, 0))
    else:
        per = bm // POOL_HALO
        halo_spec = pl.BlockSpec((POOL_HALO, d), lambda i: (jnp.maximum(i * per - 1, 0), 0))
    return pl.pallas_call(
        functools.partial(_pool_kernel, bm=bm, blocks_per_seq=blocks_per_seq, halo_is_state=halo_is_state, pos0=pos0),
        grid=(m // bm,),
        in_specs=[pl.BlockSpec((bm, d), lambda i: (i, 0)),
                  halo_spec,
                  pl.BlockSpec((1, d), lambda i: (0, 0)),
                  pl.BlockSpec(w_pool.shape, lambda i: (0, 0, 0)),
                  pl.BlockSpec((1, d), lambda i: (0, 0))],
        out_specs=pl.BlockSpec((bm, d), lambda i: (i, 0)),
        out_shape=jax.ShapeDtypeStruct((m, d), F32),
        scratch_shapes=[pltpu.VMEM((POOL_HALO + bm, d), F32)],
        compiler_params=_cparams(("parallel",)),
        name="pool_layer",
    )(x, halo_src, gain.reshape(1, d), w_pool, pool_scale.reshape(1, d))


def _sortable_key(x):
    bits = pltpu.bitcast(x, I32)
    return jnp.where(bits < 0, bits ^ jnp.int32(0x7FFFFFFF), bits)


def _lane_chunks(x):
    return [x[:, c * LANES:(c + 1) * LANES] for c in range(x.shape[1] // LANES)]


def _topk_bias(keys_ref, bias_ref, tie_ref, n_blocks, blk, q_pos, topk):
    rows = q_pos.shape[0]

    def count(pred):
        def body(kb, cnt):
            s0 = pl.multiple_of(kb * blk, blk)
            keys = keys_ref[:, pl.ds(s0, blk)]
            for c, kc in enumerate(_lane_chunks(keys)):
                pos = s0 + c * LANES + lax.broadcasted_iota(I32, (rows, LANES), 1)
                cnt = cnt + jnp.where(pred(kc, pos), 1.0, 0.0)
            return cnt
        cnt = lax.fori_loop(0, n_blocks, body, jnp.zeros((rows, LANES), F32))
        return jnp.sum(cnt, axis=-1, keepdims=True)

    def count_ge(cand):
        return count(lambda kc, pos: kc >= cand)

    thr = jnp.full((rows, 1), INT_MIN, I32)
    thr = jnp.where(count_ge(jnp.zeros((rows, 1), I32)) >= topk, 0, thr)

    def bit_body(i, thr):
        cand = thr | lax.shift_left(jnp.int32(1), 30 - i)
        return jnp.where(count_ge(cand) >= topk, cand, thr)

    thr = lax.fori_loop(0, 31, bit_body, thr)
    n_ge = count_ge(thr)
    n_gt = count(lambda kc, pos: kc > thr)
    need = topk - n_gt
    tie_ref[...] = jnp.full((rows, 1), 2 ** 30, I32)

    @pl.when(jnp.max(n_ge) > topk)
    def _():
        n_bits = max(1, int(math.ceil(math.log2(keys_ref.shape[1] + 1))))

        def cut_body(i, lo):
            cand = lo + lax.shift_left(jnp.int32(1), n_bits - 1 - i)
            f = count(lambda kc, pos: (kc == thr) & (pos <= cand))
            return jnp.where(f < need, cand, lo)

        lo = lax.fori_loop(0, n_bits, cut_body, jnp.full((rows, 1), -1, I32))
        tie_ref[...] = lo + 1

    cut = tie_ref[...]

    def write_body(kb, carry):
        s0 = pl.multiple_of(kb * blk, blk)
        keys = keys_ref[:, pl.ds(s0, blk)]
        pos = s0 + lax.broadcasted_iota(I32, (rows, blk), 1)
        sel = ((keys > thr) | ((keys == thr) & (pos <= cut))) & (pos <= q_pos)
        bias_ref[:, pl.ds(s0, blk)] = jnp.where(sel, 0.0, NEG_BIG)
        return carry

    lax.fori_loop(0, n_blocks, write_body, 0)


def _online_softmax_blocks(qs, ks, vs, bias, state):
    n = len(qs)
    ss = [lax.dot_general(qs[i], ks[i], (((1,), (1,)), ((), ())), preferred_element_type=F32) for i in range(n)]
    ms, ls, scales, ps = [], [], [], []
    for i in range(n):
        m, l, _ = state[i]
        s = ss[i] * (ATTN_SCALE * LOG2E) + bias
        m_new = jnp.maximum(m, jnp.max(s, axis=-1, keepdims=True))
        a = jnp.exp2(m - m_new)
        p = jnp.exp2(s - m_new)
        ms.append(m_new)
        ls.append(a * l + jnp.sum(p, axis=-1, keepdims=True))
        scales.append(a)
        ps.append(p.astype(BF16))
    return tuple((ms[i], ls[i], scales[i] * state[i][2] + jnp.dot(ps[i], vs[i], preferred_element_type=F32))
                 for i in range(n))


def _online_softmax_block(q, k, v, bias, m, l, acc):
    return _online_softmax_blocks([q], [k], [v], bias, ((m, l, acc),))[0]


def _dsa_mask_prompt_kernel(iq_ref, ik_ref, iw_ref, bias_ref, keys_ref, tie_ref, iwb_ref, *, bq, topk):
    qi = pl.program_id(1)
    n_blocks = qi + 1
    q_pos = qi * bq + lax.broadcasted_iota(I32, (bq, 1), 0)
    iw = iw_ref[...]
    for hh in range(N_IDX_HEADS):
        iwb_ref[hh] = jnp.broadcast_to(iw[:, hh:hh + 1], (bq, LANES))

    def score_body(kb, carry):
        s0 = pl.multiple_of(kb * bq, bq)
        ik = ik_ref[pl.ds(s0, bq), :]
        sc = jnp.zeros((bq, bq), F32)
        for hh in range(N_IDX_HEADS):
            d = lax.dot_general(iq_ref[:, hh * IDX_DIM:(hh + 1) * IDX_DIM], ik, (((1,), (1,)), ((), ())),
                                preferred_element_type=F32)
            sc = sc + jnp.maximum(d, 0.0) * jnp.concatenate([iwb_ref[hh]] * (bq // LANES), axis=-1)
        pos = s0 + lax.broadcasted_iota(I32, (bq, bq), 1)
        sc = jnp.where(pos <= q_pos, sc, -jnp.inf)
        keys_ref[:, pl.ds(s0, bq)] = _sortable_key(sc)
        return carry

    lax.fori_loop(0, n_blocks, score_body, 0)
    _topk_bias(keys_ref, bias_ref, tie_ref, n_blocks, bq, q_pos, topk)

    def mask_rest(kb, carry):
        bias_ref[:, pl.ds(pl.multiple_of(kb * bq, bq), bq)] = jnp.full((bq, bq), NEG_BIG, F32)
        return carry

    lax.fori_loop(n_blocks, pl.num_programs(1), mask_rest, 0)


def dsa_mask_prompt(iq, ik, iw, batch, t):
    bq = _pick(t, (256, 128))
    nq = t // bq
    topk = min(TOPK_MAX, t // 4)
    n_iq = N_IDX_HEADS * IDX_DIM
    return pl.pallas_call(
        functools.partial(_dsa_mask_prompt_kernel, bq=bq, topk=topk),
        grid=(batch, nq),
        in_specs=[pl.BlockSpec((bq, n_iq), lambda b, i: (b * nq + i, 0)),
                  pl.BlockSpec((t, IDX_DIM), lambda b, i: (b, 0)),
                  pl.BlockSpec((bq, N_IDX_HEADS), lambda b, i: (b * nq + i, 0))],
        out_specs=pl.BlockSpec((bq, t), lambda b, i: (b * nq + i, 0)),
        out_shape=jax.ShapeDtypeStruct((batch * t, t), F32),
        scratch_shapes=[pltpu.VMEM((bq, t), I32), pltpu.VMEM((bq, 1), I32),
                        pltpu.VMEM((N_IDX_HEADS, bq, LANES), F32)],
        compiler_params=_cparams(("parallel", "arbitrary")),
        name="dsa_mask_prompt",
    )(iq, ik, iw)


def _dsa_prompt_kernel(q_ref, k_ref, v_ref, bias_ref, o_ref, *, bq, bk, hg):
    n_blocks = ((pl.program_id(2) + 1) * bq + bk - 1) // bk
    lanes = [slice(h * HEAD_DIM, (h + 1) * HEAD_DIM) for h in range(hg)]
    qs = [q_ref[:, ln] for ln in lanes]

    def body(kb, state):
        s0 = pl.multiple_of(kb * bk, bk)
        return _online_softmax_blocks(qs, [k_ref[pl.ds(s0, bk), ln] for ln in lanes],
                                      [v_ref[pl.ds(s0, bk), ln] for ln in lanes], bias_ref[:, pl.ds(s0, bk)], state)

    state = tuple((jnp.full((bq, 1), NEG_BIG, F32), jnp.zeros((bq, 1), F32), jnp.zeros((bq, HEAD_DIM), F32))
                  for _ in range(hg))
    state = lax.fori_loop(0, n_blocks, body, state)
    for h in range(hg):
        o_ref[:, lanes[h]] = (state[h][2] / state[h][1]).astype(o_ref.dtype)


def dsa_attention_prompt(q, k, v, bias, batch, t):
    bq = _pick(t, (256, 128))
    nq = t // bq
    hg = DSA_HEADS_PER_STEP
    bk = _pick(t, (DSA_KEY_BLOCK, bq))
    return pl.pallas_call(
        functools.partial(_dsa_prompt_kernel, bq=bq, bk=bk, hg=hg),
        grid=(batch, N_HEADS // hg, nq),
        in_specs=[pl.BlockSpec((bq, hg * HEAD_DIM), lambda b, h, i: (b * nq + i, h)),
                  pl.BlockSpec((t, hg * HEAD_DIM), lambda b, h, i: (b, h)),
                  pl.BlockSpec((t, hg * HEAD_DIM), lambda b, h, i: (b, h)),
                  pl.BlockSpec((bq, t), lambda b, h, i: (b * nq + i, 0))],
        out_specs=pl.BlockSpec((bq, hg * HEAD_DIM), lambda b, h, i: (b * nq + i, h)),
        out_shape=jax.ShapeDtypeStruct((batch * t, D_MODEL), BF16),
        compiler_params=_cparams(("parallel", "parallel", "arbitrary")),
        name="dsa_attention_prompt",
    )(q, k, v, bias)


def _idx_score_sample_kernel(pt_ref, iq_ref, iw_ref, *refs, t, pages):
    ik_refs, ikn_ref, o_ref = refs[:pages], refs[pages], refs[pages + 1]
    j = pl.program_id(1)

    def scores(ik):
        d = lax.dot_general(iq_ref[...], ik.astype(BF16), (((1,), (1,)), ((), ())), preferred_element_type=F32)
        wd = jnp.maximum(d, 0.0) * iw_ref[...]
        return jnp.sum(wd.reshape(t, N_IDX_HEADS, PAGE_SIZE), axis=1)

    @pl.when(j < pl.num_programs(1) - 1)
    def _():
        for r in range(pages):
            o_ref[:, r * PAGE_SIZE:(r + 1) * PAGE_SIZE] = scores(ik_refs[r][...])

    @pl.when(j == pl.num_programs(1) - 1)
    def _():
        o_ref[:, 0:PAGE_SIZE] = scores(ikn_ref[...])
        if pages > 1:
            o_ref[:, PAGE_SIZE:] = jnp.zeros((t, (pages - 1) * PAGE_SIZE), F32)


def idx_scores_sample(iq, iw, ik_new, cache_kidx, page_table, db, t):
    n_pages = page_table.shape[1]
    pages = _pick(n_pages, (IDX_PAGES_PER_STEP, 8, 4, 2, 1))
    n_steps = n_pages // pages
    rows = t * N_IDX_HEADS
    iq_rows = iq.reshape(db, rows, IDX_DIM)
    iw_rows = iw.reshape(db, rows, 1)
    ikn = jnp.pad(ik_new.reshape(db, t, IDX_DIM), ((0, 0), (0, PAGE_SIZE - t), (0, 0)))
    page_specs = [pl.BlockSpec((None, PAGE_SIZE, IDX_DIM),
                               lambda b, j, pt, r=r: (pt[b, jnp.minimum(j, n_steps - 1) * pages + r], 0, 0))
                  for r in range(pages)]
    grid_spec = pltpu.PrefetchScalarGridSpec(
        num_scalar_prefetch=1,
        grid=(db, n_steps + 1),
        in_specs=[pl.BlockSpec((None, rows, IDX_DIM), lambda b, j, pt: (b, 0, 0)),
                  pl.BlockSpec((None, rows, 1), lambda b, j, pt: (b, 0, 0))]
        + page_specs
        + [pl.BlockSpec((None, PAGE_SIZE, IDX_DIM), lambda b, j, pt: (b, 0, 0))],
        out_specs=pl.BlockSpec((None, t, pages * PAGE_SIZE), lambda b, j, pt: (b, 0, j)),
    )
    return pl.pallas_call(
        functools.partial(_idx_score_sample_kernel, t=t, pages=pages),
        grid_spec=grid_spec,
        out_shape=jax.ShapeDtypeStruct((db, t, (n_pages + pages) * PAGE_SIZE), F32),
        compiler_params=_cparams(("parallel", "arbitrary")),
        name="idx_scores_sample",
    )(page_table, iq_rows, iw_rows, *([cache_kidx] * pages), ikn)


def _topk_sample_kernel(s_ref, qpos_ref, bias_ref, keys_ref, tie_ref, *, topk, n_blocks):
    q_pos = qpos_ref[...]
    rows = q_pos.shape[0]

    def fill(kb, carry):
        s0 = pl.multiple_of(kb * LANES, LANES)
        pos = s0 + lax.broadcasted_iota(I32, (rows, LANES), 1)
        sc = jnp.where(pos <= q_pos, s_ref[:, pl.ds(s0, LANES)], -jnp.inf)
        keys_ref[:, pl.ds(s0, LANES)] = _sortable_key(sc)
        return carry

    lax.fori_loop(0, n_blocks, fill, 0)
    _topk_bias(keys_ref, bias_ref, tie_ref, n_blocks, LANES, q_pos, topk)


def topk_bias_sample(scores, q_pos, topk):
    r, l = scores.shape
    return pl.pallas_call(
        functools.partial(_topk_sample_kernel, topk=topk, n_blocks=l // LANES),
        out_shape=jax.ShapeDtypeStruct((r, l), F32),
        scratch_shapes=[pltpu.VMEM((r, l), I32), pltpu.VMEM((r, 1), I32)],
        compiler_params=pltpu.CompilerParams(vmem_limit_bytes=VMEM_LIMIT_BYTES),
        name="topk_bias_sample",
    )(scores, q_pos)


def _dsa_sample_kernel(pt_ref, q_ref, bias_ref, *refs, t, pages):
    n_cache = 2 * N_HEAD_GROUPS * pages
    cache_refs = refs[:n_cache]
    kn_ref, vn_ref, o_ref, m_ref, l_ref, acc_ref = refs[n_cache:]
    j = pl.program_id(1)
    rows = t * N_HEADS
    q = q_ref[...]

    @pl.when(j == 0)
    def _():
        m_ref[...] = jnp.full((rows, 1), NEG_BIG, F32)
        l_ref[...] = jnp.zeros((rows, 1), F32)
        acc_ref[...] = jnp.zeros((rows, D_MODEL), F32)

    def head_rows(bias):
        return jnp.broadcast_to(bias[:, None, :], (t, N_HEADS, bias.shape[1])).reshape(rows, bias.shape[1])

    @pl.when(j < pl.num_programs(1) - 1)
    def _():
        k = jnp.concatenate([_load_page(_page_refs(cache_refs, r)[0]) for r in range(pages)], axis=0)
        v = jnp.concatenate([_load_page(_page_refs(cache_refs, r)[1]) for r in range(pages)], axis=0)
        m_ref[...], l_ref[...], acc_ref[...] = _online_softmax_block(
            q, k, v, head_rows(bias_ref[...]), m_ref[...], l_ref[...], acc_ref[...])

    @pl.when(j == pl.num_programs(1) - 1)
    def _():
        m, l, acc = _online_softmax_block(q, kn_ref[...].astype(BF16), vn_ref[...].astype(BF16),
                                          head_rows(bias_ref[:, 0:PAGE_SIZE]), m_ref[...], l_ref[...], acc_ref[...])
        o_ref[...] = _extract_block_diag(acc / l, t)


def dsa_attention_sample(q, bias, k_new, v_new, cache_k, cache_v, page_table, db, t):
    n_pages = page_table.shape[1]
    pages = _pick(n_pages, (SAMPLE_PAGES_PER_STEP, 2, 1))
    n_steps = n_pages // pages
    assert bias.shape[2] >= (n_steps + 1) * pages * PAGE_SIZE
    rows = t * N_HEADS
    qbd = _block_diag_queries(q, db, t)

    def page_of(b, j, pt, r):
        return pt[b, jnp.minimum(j, n_steps - 1) * pages + r]

    grid_spec = pltpu.PrefetchScalarGridSpec(
        num_scalar_prefetch=1,
        grid=(db, n_steps + 1),
        in_specs=[pl.BlockSpec((None, rows, D_MODEL), lambda b, j, pt: (b, 0, 0)),
                  pl.BlockSpec((None, t, pages * PAGE_SIZE), lambda b, j, pt: (b, 0, j))]
        + _cache_specs(pages, page_of)
        + [pl.BlockSpec((None, PAGE_SIZE, D_MODEL), lambda b, j, pt: (b, 0, 0)),
           pl.BlockSpec((None, PAGE_SIZE, D_MODEL), lambda b, j, pt: (b, 0, 0))],
        out_specs=pl.BlockSpec((None, t, D_MODEL), lambda b, j, pt: (b, 0, 0)),
        scratch_shapes=[pltpu.VMEM((rows, 1), F32), pltpu.VMEM((rows, 1), F32), pltpu.VMEM((rows, D_MODEL), F32)],
    )
    out = pl.pallas_call(
        functools.partial(_dsa_sample_kernel, t=t, pages=pages),
        grid_spec=grid_spec,
        out_shape=jax.ShapeDtypeStruct((db, t, D_MODEL), F32),
        compiler_params=_cparams(("parallel", "arbitrary")),
        name="dsa_attention_sample",
    )(page_table, qbd, bias, *_cache_args(cache_k, cache_v, pages), _pad_new_page(k_new, db, t),
      _pad_new_page(v_new, db, t))
    return out.reshape(db * t, D_MODEL)


def _ffn(x, f_norm, wg, wu, wd):
    h = rmsnorm_rows(x, f_norm)
    return matmul_residual(ffn_up(h, wg, wu), wd, x)


def _sb_projections(x, attn_norm, w_qkv, q_norm, k_norm):
    h = rmsnorm_rows(x, attn_norm)
    (q,) = project(h, w_qkv, 0, D_MODEL, (BF16,), gain=q_norm)
    k32, k16 = project(h, w_qkv, D_MODEL, D_MODEL, (F32, BF16), gain=k_norm)
    v32, v16 = project(h, w_qkv, 2 * D_MODEL, D_MODEL, (F32, BF16))
    return q, k32, k16, v32, v16


def _dsa_projections(x, attn_norm, w_in, w_small, q_norm, k_norm, rope_qk, rope_idx):
    h = rmsnorm_rows(x, attn_norm)
    (q,) = project(h, w_in, 0, D_MODEL, (BF16,), gain=q_norm, rope=rope_qk, rope_shift=ROT_DIM // 2)
    k32, k16 = project(h, w_in, D_MODEL, D_MODEL, (F32, BF16), gain=k_norm, rope=rope_qk, rope_shift=ROT_DIM // 2)
    v32, v16 = project(h, w_in, 2 * D_MODEL, D_MODEL, (F32, BF16))
    (iq,) = project(h, w_in, 3 * D_MODEL, N_IDX_HEADS * IDX_DIM, (BF16,), rope=rope_idx, rope_shift=IDX_ROT_DIM // 2)
    (tail_rot,) = project(h, w_small, 0, LANES, (F32,), rope=rope_idx, rope_shift=IDX_ROT_DIM // 2)
    (tail_raw,) = project(h, w_small, 0, LANES, (F32,))
    ik32 = tail_rot[:, :IDX_DIM]
    iw = tail_raw[:, IDX_DIM:IDX_DIM + N_IDX_HEADS]
    return q, k32, k16, v32, v16, iq, ik32, iw


def kernel(x_prompt, x_sample, cache_k_0, cache_v_0, state_pool_1, cache_k_2, cache_v_2, cache_kidx_2, cache_k_3, cache_v_3, page_table, attn_norm_0, w_qkv_0, q_norm_0, k_norm_0, w_o_0, ffn_norm_0, w_gate_0, w_up_0, w_down_0, attn_norm_1, w_pool_1, pool_scale_1, ffn_norm_1, w_gate_1, w_up_1, w_down_1, attn_norm_2, w_in_2, q_norm_2, k_norm_2, w_o_2, ffn_norm_2, w_gate_2, w_up_2, w_down_2, attn_norm_3, w_qkv_3, q_norm_3, k_norm_3, w_o_3, ffn_norm_3, w_gate_3, w_up_3, w_down_3):
    batch, t, d = x_prompt.shape
    db, dt, _ = x_sample.shape
    n_phys = cache_k_0.shape[0]
    n_pages = page_table.shape[1]
    past = n_pages * PAGE_SIZE
    mp, ms = batch * t, db * dt
    xp = x_prompt.reshape(mp, d)
    xs = x_sample.reshape(ms, d)

    def bf(w):
        return w.astype(BF16)

    def heads(a, b_, t_):
        return a.reshape(b_, t_, N_HEADS, HEAD_DIM)

    def sb_layer(xp, xs, cache_k, cache_v, attn_norm, w_qkv, q_norm, k_norm, w_o):
        q, k32, k16, v32, v16 = _sb_projections(xp, attn_norm, w_qkv, q_norm, k_norm)
        xp = matmul_residual(sb_attention_prompt(q, k16, v16, batch, t), w_o, xp)
        qs, ks32, _, vs32, _ = _sb_projections(xs, attn_norm, w_qkv, q_norm, k_norm)
        o_s = sb_attention_sample(qs, ks32, vs32, cache_k, cache_v, page_table, db, dt)
        xs = matmul_residual(o_s.astype(BF16), w_o, xs)
        return xp, xs, (heads(k32, batch, t), heads(v32, batch, t)), (heads(ks32, db, dt), heads(vs32, db, dt))

    xp, xs, (k0_p, v0_p), (k0_s, v0_s) = sb_layer(xp, xs, cache_k_0, cache_v_0, attn_norm_0, w_qkv_0, q_norm_0,
                                                   k_norm_0, w_o_0)
    xp = _ffn(xp, ffn_norm_0, w_gate_0, w_up_0, w_down_0)
    xs = _ffn(xs, ffn_norm_0, w_gate_0, w_up_0, w_down_0)

    w_pool = bf(w_pool_1)
    pool1_p = rmsnorm_rows(xp.reshape(batch, t, d)[:, t - POOL_HALO:].reshape(batch * POOL_HALO, d), attn_norm_1,
                           F32).reshape(batch, POOL_HALO, d)[:, POOL_HALO - POOL_STATE:]
    hs_new = rmsnorm_rows(xs, attn_norm_1, F32)
    pool1_s = jnp.concatenate([state_pool_1, hs_new.reshape(db, dt, d)], axis=1)[:, -POOL_STATE:]
    xp = pool_layer(xp, xp, attn_norm_1, w_pool, pool_scale_1, seq_len=t, halo_is_state=False, pos0=0)
    rows_s = 8
    xs_pad = jnp.pad(xs.reshape(db, dt, d), ((0, 0), (0, rows_s - dt), (0, 0))).reshape(db * rows_s, d)
    state_pad = jnp.pad(state_pool_1, ((0, 0), (POOL_HALO - POOL_STATE, 0), (0, 0))).reshape(db * POOL_HALO, d)
    xs = pool_layer(xs_pad, state_pad, attn_norm_1, w_pool, pool_scale_1, seq_len=rows_s, halo_is_state=True,
                    pos0=PAST_LEN).reshape(db, rows_s, d)[:, :dt].reshape(ms, d)
    xp = _ffn(xp, ffn_norm_1, w_gate_1, w_up_1, w_down_1)
    xs = _ffn(xs, ffn_norm_1, w_gate_1, w_up_1, w_down_1)

    w_in, w_o2 = w_in_2, w_o_2
    n_main = 3 * D_MODEL + N_IDX_HEADS * IDX_DIM
    w_small = jnp.pad(w_in[:, n_main:], ((0, 0), (0, LANES - (w_in.shape[1] - n_main))))
    pos_p = jnp.arange(t, dtype=I32)
    pos_s = jnp.tile(past + jnp.arange(dt, dtype=I32), db)
    q, k32, k16, v32, v16, iq, ik32, iw = _dsa_projections(
        xp, attn_norm_2, w_in, w_small, q_norm_2, k_norm_2,
        rope_tables(pos_p, ROT_DIM, HEAD_DIM), rope_tables(pos_p, IDX_ROT_DIM, IDX_DIM))
    iw_scaled = iw * (IDX_W_SCALE * IDX_SCALE)
    o_p = dsa_attention_prompt(q, k16, v16, dsa_mask_prompt(iq, ik32.astype(BF16), iw_scaled, batch, t), batch, t)
    k2_p, v2_p, kidx2_p = heads(k32, batch, t), heads(v32, batch, t), ik32.reshape(batch, t, IDX_DIM)
    xp = matmul_residual(o_p, w_o2, xp)

    qs, ks32, _, vs32, _, iqs, iks32, iws = _dsa_projections(
        xs, attn_norm_2, w_in, w_small, q_norm_2, k_norm_2,
        rope_tables(pos_s, ROT_DIM, HEAD_DIM), rope_tables(pos_s, IDX_ROT_DIM, IDX_DIM))
    scores = idx_scores_sample(iqs, iws * (IDX_W_SCALE * IDX_SCALE), iks32, cache_kidx_2, page_table, db, dt)
    topk_s = min(TOPK_MAX, (past + dt) // 4)
    bias_s = topk_bias_sample(scores.reshape(ms, -1), pos_s.reshape(ms, 1), topk_s).reshape(db, dt, -1)
    o_s = dsa_attention_sample(qs, bias_s, ks32, vs32, cache_k_2, cache_v_2, page_table, db, dt)
    k2_s, v2_s, kidx2_s = heads(ks32, db, dt), heads(vs32, db, dt), iks32.reshape(db, dt, IDX_DIM)
    xs = matmul_residual(o_s.astype(BF16), w_o2, xs)
    xp = _ffn(xp, ffn_norm_2, w_gate_2, w_up_2, w_down_2)
    xs = _ffn(xs, ffn_norm_2, w_gate_2, w_up_2, w_down_2)

    xp, xs, (k3_p, v3_p), (k3_s, v3_s) = sb_layer(xp, xs, cache_k_3, cache_v_3, attn_norm_3, w_qkv_3, q_norm_3,
                                                   k_norm_3, w_o_3)
    xp = _ffn(xp, ffn_norm_3, w_gate_3, w_up_3, w_down_3)
    xs = _ffn(xs, ffn_norm_3, w_gate_3, w_up_3, w_down_3)

    return (xp.reshape(batch, t, d), xs.reshape(db, dt, d), k0_p, v0_p, k0_s, v0_s, pool1_p, pool1_s,
            k2_p, v2_p, kidx2_p, k2_s, v2_s, kidx2_s, k3_p, v3_p, k3_s, v3_s)
```
